```python
import math
import jax, jax.numpy as jnp
from jax import lax
import numpy as np

D_MODEL = 1024
BATCH = 8
SEQ = 2048
DEPTH = 4
DEC_BATCH = 128
DEC_SEQ = 1
PAST_LEN = 16384
PAGE_SIZE = 128

D_MIX = D_MODEL
POOL_WIDTH = D_MIX // 4
POOL_WINDOWS = (2, 4, 8, 16)
POOL_GROUPS = len(POOL_WINDOWS)
POOL_GROUP_DIM = POOL_WIDTH // POOL_GROUPS
POOL_BUF = max(POOL_WINDOWS) - 1
DN_WIDTH = D_MIX // 2
DN_HEAD_DIM = 128
DN_HEADS = DN_WIDTH // DN_HEAD_DIM
DN_CONV = 4
DN_CHUNK = 64
CONF_WIDTH = D_MIX - POOL_WIDTH - DN_WIDTH
CONF_HEADS = 4
CONF_HEAD_DIM = CONF_WIDTH // CONF_HEADS
CONF_WIDTH_K = 31
D_FF = 4 * D_MODEL
EPS = 1e-6
OFF_POOL = 0
OFF_QKV = OFF_POOL + POOL_WIDTH
OFF_Z = OFF_QKV + 3 * DN_WIDTH
OFF_B = OFF_Z + DN_WIDTH
OFF_A = OFF_B + DN_HEADS
OFF_GLU = OFF_A + DN_HEADS
N_IN = OFF_GLU + 2 * CONF_WIDTH

kernel_name = 'hybrid_pool_gdn_conformer_decoder_step'


def _rmsnorm(x, g):
    x32 = x.astype(jnp.float32)
    y = x32 * lax.rsqrt(jnp.mean(x32 * x32, axis=-1, keepdims=True) + EPS)
    return (y * g.astype(jnp.float32)).astype(x.dtype)


def _l2norm(x):
    return x * lax.rsqrt(jnp.sum(x * x, axis=-1, keepdims=True) + EPS)


def _causal_dwconv(x_ext, w):
    ch = w.shape[1]
    return lax.conv_general_dilated(x_ext, w[:, None, :].astype(x_ext.dtype), window_strides=(1,),
                                    padding='VALID', dimension_numbers=('NWC', 'WIO', 'NWC'),
                                    feature_group_count=ch)


def _pool_mixer(u_ext, pos0, pool_w, pool_scale):
    B, L, _ = u_ext.shape
    T = L - POOL_BUF
    u32 = u_ext.astype(jnp.float32)
    cs = jnp.pad(jnp.cumsum(u32, axis=1), ((0, 0), (1, 0), (0, 0)))
    pos = pos0 + jnp.arange(T, dtype=jnp.int32)
    means = []
    for gi, w in enumerate(POOL_WINDOWS):
        lo, hi = gi * POOL_GROUP_DIM, (gi + 1) * POOL_GROUP_DIM
        s = cs[:, POOL_BUF + 1:, lo:hi] - cs[:, POOL_BUF + 1 - w:POOL_BUF + 1 - w + T, lo:hi]
        cnt = jnp.minimum(pos + 1, w).astype(jnp.float32)[None, :, None]
        means.append(s / cnt)
    d = (jnp.concatenate(means, axis=-1) - u32[:, POOL_BUF:]).reshape(B, T, POOL_GROUPS, POOL_GROUP_DIM)
    y = jnp.einsum('btgc,gcd->btgd', d, pool_w.astype(jnp.float32)).reshape(B, T, POOL_WIDTH)
    return (y * pool_scale.astype(jnp.float32)).astype(u_ext.dtype)


def _gated_delta(q, k, v, beta, g, s0):
    B, T, H, DK = q.shape
    DV = v.shape[-1]
    C = min(DN_CHUNK, T)
    n = -(-T // C)
    pad = n * C - T
    if pad:
        q, k, v = [jnp.pad(a, ((0, 0), (0, pad), (0, 0), (0, 0))) for a in (q, k, v)]
        beta, g = [jnp.pad(a, ((0, 0), (0, pad), (0, 0))) for a in (beta, g)]

    def chunks(a):
        return jnp.moveaxis(a.reshape((B, n, C) + a.shape[2:]), 3, 1)

    q, k, v, beta, g = [chunks(a) for a in (q, k, v, beta, g)]
    G = jnp.cumsum(g, axis=-1)
    idx = jnp.arange(C)
    incl = idx[:, None] >= idx[None, :]
    strict = idx[:, None] > idx[None, :]
    diff = G[..., :, None] - G[..., None, :]
    decay = jnp.where(incl, jnp.exp(jnp.where(incl, diff, 0.0)), 0.0)
    kk = jnp.einsum('bhncd,bhnsd->bhncs', k, k)
    tmat = jnp.where(strict, beta[..., :, None] * kk * decay, 0.0) + jnp.eye(C, dtype=q.dtype)
    rhs = jnp.concatenate([v * beta[..., None], k * (beta * jnp.exp(G))[..., None]], axis=-1)
    sol = lax.linalg.triangular_solve(tmat, rhs, left_side=True, lower=True, unit_diagonal=True)
    u_base, w_cum = sol[..., :DV], sol[..., DV:]
    qk = jnp.einsum('bhncd,bhnsd->bhncs', q, k) * decay
    q_g = q * jnp.exp(G)[..., None]
    k_tail = k * jnp.exp(G[..., -1:] - G)[..., None]
    g_last = jnp.exp(G[..., -1])

    def step(S, inp):
        u_b, w_c, qk_c, qg_c, kt_c, gl_c = inp
        u = u_b - jnp.einsum('bhcd,bhdv->bhcv', w_c, S)
        o = jnp.einsum('bhcd,bhdv->bhcv', qg_c, S) + jnp.einsum('bhcs,bhsv->bhcv', qk_c, u)
        S = S * gl_c[..., None, None] + jnp.einsum('bhcd,bhcv->bhdv', kt_c, u)
        return S, o

    xs = tuple(jnp.moveaxis(a, 2, 0) for a in (u_base, w_cum, qk, q_g, k_tail, g_last))
    s_new, o = lax.scan(step, s0, xs)
    o = o.transpose(1, 0, 3, 2, 4).reshape(B, n * C, H, DV)[:, :T]
    return o, s_new


def _layer(x, c, pos0, pool_buf, conv_buf, s0, conf_buf, lp):
    (w_ada, b_ada, g_norm1, g_norm2, w_in, pool_w, pool_scale, qkv_conv_w, a_log, dt_bias,
     dn_norm_g, conf_dw_w, conf_dw_b, conf_ln_g, conf_ln_b, conf_pw_w, w_out, w_ff1, w_ff2) = lp
    f32 = jnp.float32
    B, T, _ = x.shape
    mod = jax.nn.silu(c) @ w_ada + b_ada
    shift1, scale1, gate1, shift2, scale2, gate2 = [m[:, None, :] for m in jnp.split(mod, 6, axis=-1)]
    h = _rmsnorm(x, g_norm1) * (1 + scale1) + shift1
    proj = h @ w_in

    pool_ext = jnp.concatenate([pool_buf, proj[..., OFF_POOL:OFF_QKV]], axis=1)
    y_a = _pool_mixer(pool_ext, pos0, pool_w, pool_scale)
    new_pool = pool_ext[:, -POOL_BUF:]

    qkv_ext = jnp.concatenate([conv_buf, proj[..., OFF_QKV:OFF_Z]], axis=1)
    qkv = jax.nn.silu(_causal_dwconv(qkv_ext, qkv_conv_w)).astype(f32)
    new_conv = qkv_ext[:, -(DN_CONV - 1):]
    q, k, v = [a.reshape(B, T, DN_HEADS, DN_HEAD_DIM) for a in jnp.split(qkv, 3, axis=-1)]
    q = _l2norm(q) * (DN_HEAD_DIM ** -0.5)
    k = _l2norm(k)
    beta = jax.nn.sigmoid(proj[..., OFF_B:OFF_A].astype(f32))
    g = -jnp.exp(a_log.astype(f32)) * jax.nn.softplus(proj[..., OFF_A:OFF_GLU].astype(f32) + dt_bias.astype(f32))
    o, s_new = _gated_delta(q, k, v, beta, g, s0.astype(f32))
    z = proj[..., OFF_Z:OFF_B].astype(f32).reshape(B, T, DN_HEADS, DN_HEAD_DIM)
    o = o * lax.rsqrt(jnp.mean(o * o, axis=-1, keepdims=True) + EPS) * dn_norm_g.astype(f32) * jax.nn.silu(z)
    y_b = o.reshape(B, T, DN_WIDTH).astype(x.dtype)

    glu = proj[..., OFF_GLU:OFF_GLU + CONF_WIDTH] * jax.nn.sigmoid(proj[..., OFF_GLU + CONF_WIDTH:])
    conf_ext = jnp.concatenate([conf_buf, glu], axis=1)
    new_conf = conf_ext[:, -(CONF_WIDTH_K - 1):]
    dc = (_causal_dwconv(conf_ext, conf_dw_w) + conf_dw_b).astype(f32).reshape(B, T, CONF_HEADS, CONF_HEAD_DIM)
    mu = jnp.mean(dc, axis=-1, keepdims=True)
    var = jnp.mean(jnp.square(dc - mu), axis=-1, keepdims=True)
    dn = ((dc - mu) * lax.rsqrt(var + EPS)).reshape(B, T, CONF_WIDTH) * conf_ln_g.astype(f32) + conf_ln_b.astype(f32)
    y_c = jax.nn.silu(dn).astype(x.dtype) @ conf_pw_w

    mix = jnp.concatenate([y_a, y_b, y_c], axis=-1) @ w_out
    x = x + gate1 * mix

    hf = _rmsnorm(x, g_norm2) * (1 + scale2) + shift2
    a = jax.nn.relu(hf @ w_ff1)
    x = x + gate2 * ((a * a) @ w_ff2)
    return x, new_pool, new_conv, s_new, new_conf


def setup_inputs(seed: int = 0) -> dict:
    key = jax.random.key(seed)
    ks = iter(jax.random.split(key, 40))

    def nrm(shape, std):
        return jax.random.normal(next(ks), shape, jnp.float32) * std

    def gain(shape):
        return 1.0 + nrm(shape, 0.02)

    x_prompt = nrm((BATCH, SEQ, D_MODEL), 1.0)
    x_sample = nrm((DEC_BATCH, DEC_SEQ, D_MODEL), 1.0)
    state_pool = nrm((DEPTH, DEC_BATCH, POOL_BUF, POOL_WIDTH), 1.0)
    state_qkv_conv = nrm((DEPTH, DEC_BATCH, DN_CONV - 1, 3 * DN_WIDTH), 1.0)
    state_delta = nrm((DEPTH, DEC_BATCH, DN_HEADS, DN_HEAD_DIM, DN_HEAD_DIM), 0.1)
    state_conv = nrm((DEPTH, DEC_BATCH, CONF_WIDTH_K - 1, CONF_WIDTH), 0.5)
    c_prompt = nrm((BATCH, D_MODEL), 1.0)
    c_sample = nrm((DEC_BATCH, D_MODEL), 1.0)
    a_log = jnp.log(jax.random.uniform(next(ks), (DEPTH, DN_HEADS), jnp.float32, 1.0, 16.0))
    dt = jnp.exp(jax.random.uniform(next(ks), (DEPTH, DN_HEADS), jnp.float32, math.log(1e-3), math.log(1e-1)))
    dt_bias = jnp.log(jnp.expm1(dt))
    return {
        'x_prompt': x_prompt, 'x_sample': x_sample,
        'state_pool': state_pool, 'state_qkv_conv': state_qkv_conv,
        'state_delta': state_delta, 'state_conv': state_conv,
        'c_prompt': c_prompt, 'c_sample': c_sample,
        'w_ada': nrm((DEPTH, D_MODEL, 6 * D_MODEL), 0.5 * D_MODEL ** -0.5),
        'b_ada': nrm((DEPTH, 6 * D_MODEL), 0.01),
        'g_norm1': gain((DEPTH, D_MODEL)), 'g_norm2': gain((DEPTH, D_MODEL)),
        'w_in': nrm((DEPTH, D_MODEL, N_IN), D_MODEL ** -0.5),
        'pool_w': nrm((DEPTH, POOL_GROUPS, POOL_GROUP_DIM, POOL_GROUP_DIM), POOL_GROUP_DIM ** -0.5),
        'pool_scale': gain((DEPTH, POOL_WIDTH)),
        'qkv_conv_w': nrm((DEPTH, DN_CONV, 3 * DN_WIDTH), DN_CONV ** -0.5),
        'a_log': a_log, 'dt_bias': dt_bias,
        'dn_norm_g': gain((DEPTH, DN_HEAD_DIM)),
        'conf_dw_w': nrm((DEPTH, CONF_WIDTH_K, CONF_WIDTH), CONF_WIDTH_K ** -0.5),
        'conf_dw_b': nrm((DEPTH, CONF_WIDTH), 0.02),
        'conf_ln_g': gain((DEPTH, CONF_WIDTH)), 'conf_ln_b': nrm((DEPTH, CONF_WIDTH), 0.02),
        'conf_pw_w': nrm((DEPTH, CONF_WIDTH, CONF_WIDTH), CONF_WIDTH ** -0.5),
        'w_out': nrm((DEPTH, D_MIX, D_MODEL), D_MIX ** -0.5),
        'w_ff1': nrm((DEPTH, D_MODEL, D_FF), D_MODEL ** -0.5),
        'w_ff2': nrm((DEPTH, D_FF, D_MODEL), D_FF ** -0.5),
        'g_final': gain((D_MODEL,)),
    }


def reference(x_prompt, x_sample, state_pool, state_qkv_conv, state_delta, state_conv, c_prompt, c_sample,
              w_ada, b_ada, g_norm1, g_norm2, w_in, pool_w, pool_scale, qkv_conv_w, a_log, dt_bias,
              dn_norm_g, conf_dw_w, conf_dw_b, conf_ln_g, conf_ln_b, conf_pw_w, w_out, w_ff1, w_ff2, g_final):
    bp = x_prompt.shape[0]
    dt_ = x_prompt.dtype
    zero_pool = jnp.zeros((bp, POOL_BUF, POOL_WIDTH), dt_)
    zero_conv = jnp.zeros((bp, DN_CONV - 1, 3 * DN_WIDTH), dt_)
    zero_s = jnp.zeros((bp, DN_HEADS, DN_HEAD_DIM, DN_HEAD_DIM), jnp.float32)
    zero_conf = jnp.zeros((bp, CONF_WIDTH_K - 1, CONF_WIDTH), dt_)
    xp, xs = x_prompt, x_sample
    pp, pc, ps, pf = [], [], [], []
    sp, sc, ss, sf = [], [], [], []
    for l in range(DEPTH):
        lp = (w_ada[l], b_ada[l], g_norm1[l], g_norm2[l], w_in[l], pool_w[l], pool_scale[l], qkv_conv_w[l],
              a_log[l], dt_bias[l], dn_norm_g[l], conf_dw_w[l], conf_dw_b[l], conf_ln_g[l], conf_ln_b[l],
              conf_pw_w[l], w_out[l], w_ff1[l], w_ff2[l])
        xp, n_pool, n_conv, n_s, n_conf = _layer(xp, c_prompt, 0, zero_pool, zero_conv, zero_s, zero_conf, lp)
        pp.append(n_pool); pc.append(n_conv); ps.append(n_s); pf.append(n_conf)
        xs, n_pool, n_conv, n_s, n_conf = _layer(xs, c_sample, PAST_LEN, state_pool[l], state_qkv_conv[l],
                                                 state_delta[l], state_conv[l], lp)
        sp.append(n_pool); sc.append(n_conv); ss.append(n_s); sf.append(n_conf)
    y_prompt = _rmsnorm(xp, g_final)
    y_sample = _rmsnorm(xs, g_final)
    return (y_prompt, y_sample,
            jnp.stack(pp), jnp.stack(sp),
            jnp.stack(pc), jnp.stack(sc),
            jnp.stack(ps), jnp.stack(ss),
            jnp.stack(pf), jnp.stack(sf))
```

```python
import functools

import jax
import jax.numpy as jnp
from jax import lax
from jax.experimental import pallas as pl
from jax.experimental.pallas import tpu as pltpu

F32 = jnp.float32
BF16 = jnp.bfloat16
EPS = 1e-6

D_MODEL = 1024
POOL_WIDTH = 256
POOL_WINDOWS = (2, 4, 8, 16)
POOL_BUF = 15
DN_WIDTH = 512
DN_HEAD_DIM = 128
DN_HEADS = 4
DN_CONV = 4
CONF_WIDTH = 256
CONF_HEADS = 4
CONF_K = 31
D_FF = 4 * D_MODEL
PAST_LEN = 16384
OFF_QKV = POOL_WIDTH
OFF_Z = OFF_QKV + 3 * DN_WIDTH
OFF_B = OFF_Z + DN_WIDTH
OFF_A = OFF_B + DN_HEADS
OFF_GLU = OFF_A + DN_HEADS
R_QKV = POOL_WIDTH
R_Z = R_QKV + 3 * DN_WIDTH
R_GLU = R_Z + DN_WIDTH
R_BA = R_GLU + 2 * CONF_WIDTH
R_END = R_BA + 128

LANES = 128
POOL_HIST = 16
CONF_HIST = 32
QKV_HIST = 8
GDN_CHUNK = 64
SOLVE_BASE = 16
VMEM_LIMIT = 56 * 1024 * 1024


def _cparams(sem):
    return pltpu.CompilerParams(dimension_semantics=sem, vmem_limit_bytes=VMEM_LIMIT)


def _dot(a, b):
    return jnp.dot(a.astype(BF16), b.astype(BF16), preferred_element_type=F32)


def _dot_nt(a, b):
    return lax.dot_general(a.astype(BF16), b.astype(BF16), (((1,), (1,)), ((), ())),
                           preferred_element_type=F32)


def _dot_tn(a, b):
    return lax.dot_general(a.astype(BF16), b.astype(BF16), (((0,), (0,)), ((), ())),
                           preferred_element_type=F32)


def _split_dot(x, m):
    hi = x.astype(BF16)
    lo = (x - hi.astype(F32)).astype(BF16)
    return (jnp.dot(hi, m, preferred_element_type=F32)
            + jnp.dot(lo, m, preferred_element_type=F32))


def _sigmoid(x):
    return 1.0 / (1.0 + jnp.exp(-x))


def _silu(x):
    return x * _sigmoid(x)


def _softplus(x):
    return jnp.maximum(x, 0.0) + jnp.log1p(jnp.exp(-jnp.abs(x)))


def _mod_rmsnorm(x, g, scale, shift):
    y = x * lax.rsqrt(jnp.mean(x * x, axis=-1, keepdims=True) + EPS)
    return (y * g) * (1.0 + scale) + shift


def _ada_kernel(c_ref, w_ref, b_ref, o_ref):
    o_ref[...] = _dot(_silu(c_ref[...]), w_ref[...]) + b_ref[...]


def _ada(c_all, w_ada, b_ada):
    depth, d, n = w_ada.shape
    rows = c_all.shape[0]
    tn = 1536
    return pl.pallas_call(
        _ada_kernel,
        grid=(depth, n // tn),
        in_specs=[pl.BlockSpec((rows, d), lambda l, j: (0, 0)),
                  pl.BlockSpec((None, d, tn), lambda l, j: (l, 0, j)),
                  pl.BlockSpec((None, 1, tn), lambda l, j: (l, 0, j))],
        out_specs=pl.BlockSpec((None, rows, tn), lambda l, j: (l, 0, j)),
        out_shape=jax.ShapeDtypeStruct((depth, rows, n), F32),
        compiler_params=_cparams(("arbitrary", "arbitrary")),
        name="ada_mod",
    )(c_all, w_ada, b_ada.reshape(depth, 1, n))


def _inproj_kernel(x_ref, shift_ref, scale_ref, g_ref, w_ref, alog_ref, dtb_ref,
                   pool_o, qkv_o, z_o, glu_o, bg_o):
    h = _mod_rmsnorm(x_ref[...], g_ref[...], scale_ref[...], shift_ref[...]).astype(BF16)

    def seg(lo, hi):
        return jnp.dot(h, w_ref[:, lo:hi], preferred_element_type=F32)

    pool_o[...] = seg(0, R_QKV)
    qkv_o[...] = seg(R_QKV, R_Z)
    z_o[...] = seg(R_Z, R_GLU)
    gl = seg(R_GLU, R_BA)
    glu_o[...] = gl[:, :CONF_WIDTH] * _sigmoid(gl[:, CONF_WIDTH:])
    ba = seg(R_BA, R_END)
    lane = lax.broadcasted_iota(jnp.int32, ba.shape, 1)
    beta = _sigmoid(ba)
    g = -jnp.exp(alog_ref[...]) * _softplus(ba + dtb_ref[...])
    bg_o[...] = jnp.where(lane < DN_HEADS, beta, g)


def _mod_spec(mode, chunk, tm, rows_per_batch):
    if mode == "batch":
        per = rows_per_batch // tm
        return pl.BlockSpec((None, 1, D_MODEL), lambda i: ((i // per) * 6 + chunk, 0, 0))
    return pl.BlockSpec((tm, D_MODEL), lambda i: (i, chunk))


def _inproj(x, mod, mode, layer, rows_per_batch, g1, w_in_r, alog_row, dtb_row, tm):
    n = x.shape[0]
    row = lambda w: pl.BlockSpec((tm, w), lambda i: (i, 0))
    const3 = lambda a, b: pl.BlockSpec((None, a, b), lambda i: (layer, 0, 0))
    outs = [POOL_WIDTH, 3 * DN_WIDTH, DN_WIDTH, CONF_WIDTH, LANES]
    return pl.pallas_call(
        _inproj_kernel,
        grid=(n // tm,),
        in_specs=[row(D_MODEL),
                  _mod_spec(mode,0, tm, rows_per_batch),
                  _mod_spec(mode,1, tm, rows_per_batch),
                  const3(1, D_MODEL),
                  pl.BlockSpec((None, D_MODEL, R_END), lambda i: (layer, 0, 0),
                               pipeline_mode=pl.Buffered(1)),
                  const3(1, LANES), const3(1, LANES)],
        out_specs=[row(w) for w in outs],
        out_shape=[jax.ShapeDtypeStruct((n, w), F32) for w in outs],
        compiler_params=_cparams(("parallel",)),
        name="in_proj",
    )(x, mod, mod, g1, w_in_r, alog_row, dtb_row)


def _pool_delta(win, u, pos, half):
    w_lo, w_hi = POOL_WINDOWS[2 * half], POOL_WINDOWS[2 * half + 1]
    a = win(0)
    for j in range(1, w_lo):
        a = a + win(j)
    b = win(w_lo)
    for j in range(w_lo + 1, w_hi):
        b = b + win(j)
    lane = lax.broadcasted_iota(jnp.int32, u.shape, 1)
    upper = lane >= LANES // 2
    s = a + jnp.where(upper, b, 0.0)
    cnt = jnp.where(upper, jnp.minimum(pos + 1, w_hi), jnp.minimum(pos + 1, w_lo)).astype(F32)
    return s / cnt - u


def _conf_post(dc, avg, lng, lnb, cpw):
    mu = _split_dot(dc, avg)
    xc = dc - mu
    var = _split_dot(xc * xc, avg)
    dn = xc * lax.rsqrt(var + EPS) * lng + lnb
    return _dot(_silu(dn), cpw)


def _poolconf_kernel(u_ref, glu_ref, pw_ref, ps_ref, cw_ref, cb_ref, lng_ref, lnb_ref, cpw_ref,
                     avg_ref, ya_o, yc_o, epool, econf, *, tt, sub):
    t = pl.program_id(1)

    @pl.when(t == 0)
    def _():
        epool[0:POOL_HIST, :] = jnp.zeros((POOL_HIST, POOL_WIDTH), F32)
        econf[0:CONF_HIST, :] = jnp.zeros((CONF_HIST, CONF_WIDTH), F32)

    @pl.when(t > 0)
    def _():
        epool[0:POOL_HIST, :] = epool[tt:tt + POOL_HIST, :]
        econf[0:CONF_HIST, :] = econf[tt:tt + CONF_HIST, :]

    epool[POOL_HIST:, :] = u_ref[...]
    econf[CONF_HIST:, :] = glu_ref[...]

    d_blocks, dc_blocks = [], []
    for r in range(0, tt, sub):
        pos = t * tt + r + lax.broadcasted_iota(jnp.int32, (sub, 1), 0)
        halves = []
        for half in range(2):
            cols = slice(half * LANES, (half + 1) * LANES)
            win = lambda j, cols=cols: epool[POOL_HIST + r - j:POOL_HIST + r - j + sub, cols]
            halves.append(_pool_delta(win, win(0), pos, half))
        d_blocks.append(jnp.concatenate(halves, axis=1))
        base = CONF_HIST - (CONF_K - 1) + r
        acc = cb_ref[...] + cw_ref[0:1, :] * econf[base:base + sub, :]
        for j in range(1, CONF_K):
            acc = acc + cw_ref[j:j + 1, :] * econf[base + j:base + j + sub, :]
        dc_blocks.append(acc)
    d = jnp.concatenate(d_blocks, axis=0)
    ya_o[...] = _dot(d, pw_ref[...]) * ps_ref[...]
    dc = jnp.concatenate(dc_blocks, axis=0)
    yc_o[...] = _conf_post(dc, avg_ref[...], lng_ref[...], lnb_ref[...], cpw_ref[...])


def _poolconf(u, glu, layer, batch, seq, pw_bd, pscale, cw, cb, lng, lnb, cpw, avg, tt):
    per = seq // tt
    row = pl.BlockSpec((tt, POOL_WIDTH), lambda b, t: (b * per + t, 0))
    const3 = lambda a, b_: pl.BlockSpec((None, a, b_), lambda b, t: (layer, 0, 0))
    return pl.pallas_call(
        functools.partial(_poolconf_kernel, tt=tt, sub=64),
        grid=(batch, per),
        in_specs=[row, row,
                  const3(POOL_WIDTH, POOL_WIDTH), const3(1, POOL_WIDTH),
                  const3(CONF_K, CONF_WIDTH), const3(1, CONF_WIDTH),
                  const3(1, CONF_WIDTH), const3(1, CONF_WIDTH),
                  const3(CONF_WIDTH, CONF_WIDTH),
                  pl.BlockSpec((CONF_WIDTH, CONF_WIDTH), lambda b, t: (0, 0))],
        out_specs=[row, row],
        out_shape=[jax.ShapeDtypeStruct(u.shape, F32), jax.ShapeDtypeStruct(glu.shape, F32)],
        scratch_shapes=[pltpu.VMEM((POOL_HIST + tt, POOL_WIDTH), F32),
                        pltpu.VMEM((CONF_HIST + tt, CONF_WIDTH), F32)],
        compiler_params=_cparams(("parallel", "arbitrary")),
        name="pool_conf",
    )(u, glu, pw_bd, pscale, cw, cb, lng, lnb, cpw, avg)


def _poolconf_step_kernel(u_ref, sp_ref, glu_ref, sc_ref, pw_ref, ps_ref, cw_ref, cb_ref, lng_ref,
                          lnb_ref, cpw_ref, avg_ref, ya_o, yc_o, np_o, nc_o, *, pos0):
    u = u_ref[...]
    rows = u.shape[0]
    pos = jnp.full((rows, 1), pos0, jnp.int32)
    halves = []
    for half in range(2):
        def win(j, half=half):
            if j == 0:
                return u[:, half * LANES:(half + 1) * LANES]
            lo = (POOL_BUF - j) * POOL_WIDTH + half * LANES
            return sp_ref[:, lo:lo + LANES]
        halves.append(_pool_delta(win, win(0), pos, half))
    d = jnp.concatenate(halves, axis=1)
    ya_o[...] = _dot(d, pw_ref[...]) * ps_ref[...]
    np_o[:, 0:(POOL_BUF - 1) * POOL_WIDTH] = sp_ref[:, POOL_WIDTH:]
    np_o[:, (POOL_BUF - 1) * POOL_WIDTH:] = u

    glu = glu_ref[...]
    acc = cb_ref[...] + cw_ref[CONF_K - 1:CONF_K, :] * glu
    for j in range(CONF_K - 1):
        acc = acc + cw_ref[j:j + 1, :] * sc_ref[:, j * CONF_WIDTH:(j + 1) * CONF_WIDTH]
    yc_o[...] = _conf_post(acc, avg_ref[...], lng_ref[...], lnb_ref[...], cpw_ref[...])
    nc_o[:, 0:(CONF_K - 2) * CONF_WIDTH] = sc_ref[:, CONF_WIDTH:]
    nc_o[:, (CONF_K - 2) * CONF_WIDTH:] = glu


def _poolconf_step(u, sp, glu, sc, layer, pw_bd, pscale, cw, cb, lng, lnb, cpw, avg, bt):
    n = u.shape[0]
    row = lambda w: pl.BlockSpec((bt, w), lambda i: (i, 0))
    const3 = lambda a, b_: pl.BlockSpec((None, a, b_), lambda i: (layer, 0, 0))
    widths = [POOL_WIDTH, CONF_WIDTH, sp.shape[1], sc.shape[1]]
    return pl.pallas_call(
        functools.partial(_poolconf_step_kernel, pos0=PAST_LEN),
        grid=(n // bt,),
        in_specs=[row(POOL_WIDTH), row(sp.shape[1]), row(CONF_WIDTH), row(sc.shape[1]),
                  const3(POOL_WIDTH, POOL_WIDTH), const3(1, POOL_WIDTH),
                  const3(CONF_K, CONF_WIDTH), const3(1, CONF_WIDTH),
                  const3(1, CONF_WIDTH), const3(1, CONF_WIDTH),
                  const3(CONF_WIDTH, CONF_WIDTH),
                  pl.BlockSpec((CONF_WIDTH, CONF_WIDTH), lambda i: (0, 0))],
        out_specs=[row(w) for w in widths],
        out_shape=[jax.ShapeDtypeStruct((n, w), F32) for w in widths],
        compiler_params=_cparams(("parallel",)),
        name="pool_conf_step",
    )(u, sp, glu, sc, pw_bd, pscale, cw, cb, lng, lnb, cpw, avg)


def _l2norm(x):
    return x * lax.rsqrt(jnp.sum(x * x, axis=-1, keepdims=True) + EPS)


def _gated_norm(o, ng, z):
    return o * lax.rsqrt(jnp.mean(o * o, axis=-1, keepdims=True) + EPS) * ng * _silu(z)


def _tri_masks(c):
    ri = lax.broadcasted_iota(jnp.int32, (c, c), 0)
    ci = lax.broadcasted_iota(jnp.int32, (c, c), 1)
    masks = [ri // SOLVE_BASE == ci // SOLVE_BASE]
    size = SOLVE_BASE
    while size < c:
        masks.append((ri // (2 * size) == ci // (2 * size)) & (ri // size != ci // size))
        size *= 2
    return masks


def _tri_solve(b, rhs, c, masks):
    n = jnp.where(masks[0], b, 0.0)
    p = _dot(n, n)
    power = 2
    while power < SOLVE_BASE:
        if 2 * power < SOLVE_BASE:
            st = _dot(jnp.concatenate([n, p], axis=0), p)
            n, p = n + p + st[:c], st[c:]
        else:
            n = n + p + _dot(n, p)
        power *= 2
    for m in masks[1:]:
        l = jnp.where(m, b, 0.0)
        tl = l + _dot(n, l)
        n = n + tl + _dot(tl, n)
    return rhs + _dot(n, rhs)


def _gdn_kernel(qkv_ref, z_ref, bg_ref, cw_ref, ng_ref, y_o, s_o, eq, s_scr, *, tt, chunk):
    t = pl.program_id(1)
    width = 3 * DN_WIDTH

    @pl.when(t == 0)
    def _():
        eq[0:QKV_HIST, :] = jnp.zeros((QKV_HIST, width), F32)
        s_scr[...] = jnp.zeros(s_scr.shape, F32)

    @pl.when(t > 0)
    def _():
        eq[0:QKV_HIST, :] = eq[tt:tt + QKV_HIST, :]

    eq[QKV_HIST:, :] = qkv_ref[...]

    def conv_cols(lo):
        base = QKV_HIST - (DN_CONV - 1)
        acc = cw_ref[0:1, lo:lo + LANES] * eq[base:base + tt, lo:lo + LANES]
        for j in range(1, DN_CONV):
            acc = acc + cw_ref[j:j + 1, lo:lo + LANES] * eq[base + j:base + j + tt, lo:lo + LANES]
        return _silu(acc)

    bg = bg_ref[...]
    ri = lax.broadcasted_iota(jnp.int32, (tt, tt), 0)
    ci = lax.broadcasted_iota(jnp.int32, (tt, tt), 1)
    lblk = jnp.where((ri >= ci) & (ri // chunk == ci // chunk), 1.0, 0.0).astype(BF16)
    g_hi = bg.astype(BF16)
    g_r = bg - g_hi.astype(F32)
    g_mid = g_r.astype(BF16)
    g_lo = (g_r - g_mid.astype(F32)).astype(BF16)
    gc = (jnp.dot(lblk, g_hi, preferred_element_type=F32)
          + jnp.dot(lblk, g_mid, preferred_element_type=F32)
          + jnp.dot(lblk, g_lo, preferred_element_type=F32))
    gt = gc.T

    rc = lax.broadcasted_iota(jnp.int32, (chunk, chunk), 0)
    cc = lax.broadcasted_iota(jnp.int32, (chunk, chunk), 1)
    incl = rc >= cc
    strict = rc > cc
    masks = _tri_masks(chunk)

    for h in range(DN_HEADS):
        q = _l2norm(conv_cols(h * DN_HEAD_DIM)) * (DN_HEAD_DIM ** -0.5)
        k = _l2norm(conv_cols(DN_WIDTH + h * DN_HEAD_DIM))
        v = conv_cols(2 * DN_WIDTH + h * DN_HEAD_DIM)
        beta = bg[:, h:h + 1]
        gcol = gc[:, DN_HEADS + h:DN_HEADS + h + 1]
        grow = gt[DN_HEADS + h:DN_HEADS + h + 1, :]
        eg = jnp.exp(gcol)
        rhs_v = v * beta
        rhs_k = k * (beta * eg)
        qg = q * eg
        s = s_scr[h]
        outs = []
        for c in range(tt // chunk):
            sl = slice(c * chunk, (c + 1) * chunk)
            kc = k[sl]
            kq = _dot_nt(jnp.concatenate([kc, q[sl]], axis=0), kc)
            diff = gcol[sl] - grow[:, sl]
            decay = jnp.where(incl, jnp.exp(jnp.where(incl, diff, 0.0)), 0.0)
            bmat = jnp.where(strict, -(beta[sl] * kq[:chunk] * decay), 0.0)
            qk = kq[chunk:] * decay
            rhs = jnp.concatenate([rhs_v[sl], rhs_k[sl]], axis=1)
            sol = _tri_solve(bmat, rhs, chunk, masks)
            u_base, w_cum = sol[:, :DN_HEAD_DIM], sol[:, DN_HEAD_DIM:]
            g_last = gcol[(c + 1) * chunk - 1:(c + 1) * chunk]
            k_tail = kc * jnp.exp(g_last - gcol[sl])
            ws = _dot(jnp.concatenate([w_cum, qg[sl]], axis=0), s)
            u = u_base - ws[:chunk]
            outs.append(ws[chunk:] + _dot(qk, u))
            s = s * jnp.exp(g_last) + _dot_tn(k_tail, u)
        s_scr[h] = s
        o = jnp.concatenate(outs, axis=0)
        cols = slice(h * DN_HEAD_DIM, (h + 1) * DN_HEAD_DIM)
        y_o[:, cols] = _gated_norm(o, ng_ref[...], z_ref[:, cols])

    @pl.when(t == pl.num_programs(1) - 1)
    def _():
        s_o[...] = s_scr[...]


def _gdn(qkv, z, bg, layer, batch, seq, cw, ng, tt):
    per = seq // tt
    row = lambda w: pl.BlockSpec((tt, w), lambda b, t: (b * per + t, 0))
    const3 = lambda a, b_: pl.BlockSpec((None, a, b_), lambda b, t: (layer, 0, 0))
    state = (DN_HEADS, DN_HEAD_DIM, DN_HEAD_DIM)
    return pl.pallas_call(
        functools.partial(_gdn_kernel, tt=tt, chunk=GDN_CHUNK),
        grid=(batch, per),
        in_specs=[row(3 * DN_WIDTH), row(DN_WIDTH), row(LANES),
                  const3(DN_CONV, 3 * DN_WIDTH), const3(1, DN_HEAD_DIM)],
        out_specs=[row(DN_WIDTH),
                   pl.BlockSpec((None,) + state, lambda b, t: (b, 0, 0, 0))],
        out_shape=[jax.ShapeDtypeStruct(z.shape, F32),
                   jax.ShapeDtypeStruct((batch,) + state, F32)],
        scratch_shapes=[pltpu.VMEM((QKV_HIST + tt, 3 * DN_WIDTH), F32),
                        pltpu.VMEM(state, F32)],
        compiler_params=_cparams(("parallel", "arbitrary")),
        name="gated_delta",
    )(qkv, z, bg, cw, ng)


def _gdn_step_kernel(qkv_ref, sq_ref, z_ref, bg_ref, s_ref, cw_ref, ng_ref, y_o, nq_o, s_o):
    width = 3 * DN_WIDTH
    qkv = qkv_ref[...]
    bt = qkv.shape[0]
    acc = cw_ref[DN_CONV - 1:DN_CONV, :] * qkv
    for j in range(DN_CONV - 1):
        acc = acc + cw_ref[j:j + 1, :] * sq_ref[:, j * width:(j + 1) * width]
    act = _silu(acc)
    nq_o[:, 0:(DN_CONV - 2) * width] = sq_ref[:, width:]
    nq_o[:, (DN_CONV - 2) * width:] = qkv
    bg = bg_ref[...]
    for h in range(DN_HEADS):
        cols = slice(h * DN_HEAD_DIM, (h + 1) * DN_HEAD_DIM)
        q = _l2norm(act[:, cols]) * (DN_HEAD_DIM ** -0.5)
        k = _l2norm(act[:, DN_WIDTH + h * DN_HEAD_DIM:DN_WIDTH + (h + 1) * DN_HEAD_DIM])
        v = act[:, 2 * DN_WIDTH + h * DN_HEAD_DIM:2 * DN_WIDTH + (h + 1) * DN_HEAD_DIM]
        beta = bg[:, h:h + 1]
        eg = jnp.exp(bg[:, DN_HEADS + h:DN_HEADS + h + 1])
        qk = jnp.sum(q * k, axis=-1, keepdims=True)
        outs = []
        for b in range(bt):
            s = s_ref[b, h]
            k_col = jnp.broadcast_to(k[b:b + 1], (DN_HEAD_DIM, DN_HEAD_DIM)).T
            q_col = jnp.broadcast_to(q[b:b + 1], (DN_HEAD_DIM, DN_HEAD_DIM)).T
            ks = jnp.sum(k_col * s, axis=0, keepdims=True)
            qs = jnp.sum(q_col * s, axis=0, keepdims=True)
            e = eg[b:b + 1]
            u = beta[b:b + 1] * (v[b:b + 1] - e * ks)
            outs.append(e * qs + qk[b:b + 1] * u)
            s_o[b, h] = e * s + k_col * u
        o = jnp.concatenate(outs, axis=0)
        y_o[:, cols] = _gated_norm(o, ng_ref[...], z_ref[:, cols])


def _gdn_step(qkv, sq, z, bg, s, layer, cw, ng, bt):
    n = qkv.shape[0]
    row = lambda w: pl.BlockSpec((bt, w), lambda i: (i, 0))
    const3 = lambda a, b_: pl.BlockSpec((None, a, b_), lambda i: (layer, 0, 0))
    sblk = pl.BlockSpec((bt, DN_HEADS, DN_HEAD_DIM, DN_HEAD_DIM), lambda i: (i, 0, 0, 0))
    return pl.pallas_call(
        _gdn_step_kernel,
        grid=(n // bt,),
        in_specs=[row(3 * DN_WIDTH), row(sq.shape[1]), row(DN_WIDTH), row(LANES), sblk,
                  const3(DN_CONV, 3 * DN_WIDTH), const3(1, DN_HEAD_DIM)],
        out_specs=[row(DN_WIDTH), row(sq.shape[1]), sblk],
        out_shape=[jax.ShapeDtypeStruct(z.shape, F32), jax.ShapeDtypeStruct(sq.shape, F32),
                   jax.ShapeDtypeStruct(s.shape, F32)],
        compiler_params=_cparams(("parallel",)),
        name="gated_delta_step",
    )(qkv, sq, z, bg, s, cw, ng)


def _outproj_kernel(x_ref, ya_ref, yb_ref, yc_ref, gate_ref, w_ref, o_ref):
    a_end, b_end = POOL_WIDTH, POOL_WIDTH + DN_WIDTH
    mix = (_dot(ya_ref[...], w_ref[0:a_end, :]) + _dot(yb_ref[...], w_ref[a_end:b_end, :])
           + _dot(yc_ref[...], w_ref[b_end:, :]))
    o_ref[...] = x_ref[...] + gate_ref[...] * mix


def _outproj(x, ya, yb, yc, mod, mode, layer, rows_per_batch, w_out, tm):
    n = x.shape[0]
    row = lambda w: pl.BlockSpec((tm, w), lambda i: (i, 0))
    return pl.pallas_call(
        _outproj_kernel,
        grid=(n // tm,),
        in_specs=[row(D_MODEL), row(POOL_WIDTH), row(DN_WIDTH), row(CONF_WIDTH),
                  _mod_spec(mode,2, tm, rows_per_batch),
                  pl.BlockSpec((None, D_MODEL, D_MODEL), lambda i: (layer, 0, 0),
                               pipeline_mode=pl.Buffered(1))],
        out_specs=row(D_MODEL),
        out_shape=jax.ShapeDtypeStruct(x.shape, F32),
        compiler_params=_cparams(("parallel",)),
        name="out_proj",
    )(x, ya, yb, yc, mod, w_out)


def _ffn_kernel(x_ref, shift_ref, scale_ref, gate_ref, g_ref, w1_ref, w2_ref, gf_ref, o_ref, *,
                final, hidden_chunk):
    x = x_ref[...]
    h = _mod_rmsnorm(x, g_ref[...], scale_ref[...], shift_ref[...]).astype(BF16)
    acc = jnp.zeros(x.shape, F32)
    for c in range(0, D_FF, hidden_chunk):
        a = jnp.maximum(jnp.dot(h, w1_ref[:, c:c + hidden_chunk], preferred_element_type=F32), 0.0)
        acc = acc + jnp.dot((a * a).astype(BF16), w2_ref[c:c + hidden_chunk, :],
                            preferred_element_type=F32)
    y = x + gate_ref[...] * acc
    if final:
        y = y * lax.rsqrt(jnp.mean(y * y, axis=-1, keepdims=True) + EPS) * gf_ref[...]
    o_ref[...] = y


def _ffn(x, mod, mode, layer, rows_per_batch, g2, w1, w2, g_final, final, tm):
    n = x.shape[0]
    row = pl.BlockSpec((tm, D_MODEL), lambda i: (i, 0))
    return pl.pallas_call(
        functools.partial(_ffn_kernel, final=final, hidden_chunk=1024),
        grid=(n // tm,),
        in_specs=[row,
                  _mod_spec(mode,3, tm, rows_per_batch),
                  _mod_spec(mode,4, tm, rows_per_batch),
                  _mod_spec(mode,5, tm, rows_per_batch),
                  pl.BlockSpec((None, 1, D_MODEL), lambda i: (layer, 0, 0)),
                  pl.BlockSpec((None, D_MODEL, D_FF), lambda i: (layer, 0, 0),
                               pipeline_mode=pl.Buffered(1)),
                  pl.BlockSpec((None, D_FF, D_MODEL), lambda i: (layer, 0, 0),
                               pipeline_mode=pl.Buffered(1)),
                  pl.BlockSpec((1, D_MODEL), lambda i: (0, 0))],
        out_specs=row,
        out_shape=jax.ShapeDtypeStruct(x.shape, F32),
        compiler_params=_cparams(("parallel",)),
        name="ffn",
    )(x, mod, mod, mod, g2, w1, w2, g_final)


def kernel(x_prompt, x_sample, state_pool, state_qkv_conv, state_delta, state_conv, c_prompt, c_sample,
           w_ada, b_ada, g_norm1, g_norm2, w_in, pool_w, pool_scale, qkv_conv_w, a_log, dt_bias,
           dn_norm_g, conf_dw_w, conf_dw_b, conf_ln_g, conf_ln_b, conf_pw_w, w_out, w_ff1, w_ff2, g_final):
    depth = w_in.shape[0]
    bp, seq, d = x_prompt.shape
    bs = x_sample.shape[0]
    assert d == D_MODEL and x_sample.shape[1] == 1

    ba_cols = jnp.pad(w_in[:, :, OFF_B:OFF_GLU], ((0, 0), (0, 0), (0, LANES - 2 * DN_HEADS)))
    w_in_r = jnp.concatenate([w_in[:, :, :OFF_B], w_in[:, :, OFF_GLU:], ba_cols], axis=-1).astype(BF16)
    groups = pool_w.shape[1]
    pw_bd = jnp.einsum("lgcd,gh->lgchd", pool_w, jnp.eye(groups, dtype=pool_w.dtype))
    pw_bd = pw_bd.reshape(depth, POOL_WIDTH, POOL_WIDTH).astype(BF16)
    head_of = jnp.arange(CONF_WIDTH) // (CONF_WIDTH // CONF_HEADS)
    avg = jnp.where(head_of[:, None] == head_of[None, :], CONF_HEADS / CONF_WIDTH, 0.0).astype(BF16)
    lane_pad = lambda a: jnp.pad(a, ((0, 0), (DN_HEADS, LANES - 2 * DN_HEADS)))[:, None, :]
    alog_row, dtb_row = lane_pad(a_log), lane_pad(dt_bias)
    r3 = lambda a: a[:, None, :]
    g1, g2, pscale, ng = r3(g_norm1), r3(g_norm2), r3(pool_scale), r3(dn_norm_g)
    cb, lng, lnb = r3(conf_dw_b), r3(conf_ln_g), r3(conf_ln_b)
    cpw = conf_pw_w.astype(BF16)
    w_out_b, w1_b, w2_b = w_out.astype(BF16), w_ff1.astype(BF16), w_ff2.astype(BF16)
    gf = g_final[None, :]

    mods = _ada(jnp.concatenate([c_prompt, c_sample], axis=0), w_ada, b_ada)
    mods_p = mods[:, :bp].reshape(depth, bp * 6, 1, d)
    mods_s = mods[:, bp:]

    xp = x_prompt.reshape(bp * seq, d)
    xs = x_sample.reshape(bs, d)
    tm_p, tt_mix, tt_gdn, bt_mix, bt_gdn = 512, 256, 256, min(32, bs), min(8, bs)
    outs = [[] for _ in range(8)]
    for l in range(depth):
        last = l == depth - 1
        mp = mods_p[l]
        u, qkv, z, glu, bg = _inproj(xp, mp, "batch", l, seq, g1, w_in_r, alog_row, dtb_row, tm_p)
        ya, yc = _poolconf(u, glu, l, bp, seq, pw_bd, pscale, conf_dw_w, cb, lng, lnb, cpw, avg, tt_mix)
        yb, s_new = _gdn(qkv, z, bg, l, bp, seq, qkv_conv_w, ng, tt_gdn)
        xp = _outproj(xp, ya, yb, yc, mp, "batch", l, seq, w_out_b, tm_p)
        xp = _ffn(xp, mp, "batch", l, seq, g2, w1_b, w2_b, gf, last, tm_p)
        outs[0].append(u.reshape(bp, seq, -1)[:, seq - POOL_BUF:])
        outs[2].append(qkv.reshape(bp, seq, -1)[:, seq - (DN_CONV - 1):])
        outs[4].append(s_new)
        outs[6].append(glu.reshape(bp, seq, -1)[:, seq - (CONF_K - 1):])
        ms = mods_s[l]
        u, qkv, z, glu, bg = _inproj(xs, ms, "row", l, 1, g1, w_in_r, alog_row, dtb_row, bs)
        ya, yc, n_pool, n_conf = _poolconf_step(
            u, state_pool[l].reshape(bs, -1), glu, state_conv[l].reshape(bs, -1), l,
            pw_bd, pscale, conf_dw_w, cb, lng, lnb, cpw, avg, bt_mix)
        yb, n_qkv, s_new = _gdn_step(qkv, state_qkv_conv[l].reshape(bs, -1), z, bg, state_delta[l], l,
                                     qkv_conv_w, ng, bt_gdn)
        xs = _outproj(xs, ya, yb, yc, ms, "row", l, 1, w_out_b, bs)
        xs = _ffn(xs, ms, "row", l, 1, g2, w1_b, w2_b, gf, last, bs)
        outs[1].append(n_pool.reshape(bs, POOL_BUF, POOL_WIDTH))
        outs[3].append(n_qkv.reshape(bs, DN_CONV - 1, 3 * DN_WIDTH))
        outs[5].append(s_new)
        outs[7].append(n_conf.reshape(bs, CONF_K - 1, CONF_WIDTH))
    stacked = [jnp.stack(o) for o in outs]
    return (xp.reshape(bp, seq, d), xs.reshape(bs, 1, d)) + tuple(stacked)
```

```python
import functools

import jax
import jax.numpy as jnp
from jax import lax
from jax.experimental import pallas as pl
from jax.experimental.pallas import tpu as pltpu

F32 = jnp.float32
BF16 = jnp.bfloat16
EPS = 1e-6

D_MODEL = 1024
POOL_WIDTH = 256
POOL_WINDOWS = (2, 4, 8, 16)
POOL_BUF = 15
DN_WIDTH = 512
DN_HEAD_DIM = 128
DN_HEADS = 4
DN_CONV = 4
CONF_WIDTH = 256
CONF_HEADS = 4
CONF_K = 31
D_FF = 4 * D_MODEL
PAST_LEN = 16384
OFF_QKV = POOL_WIDTH
OFF_Z = OFF_QKV + 3 * DN_WIDTH
OFF_B = OFF_Z + DN_WIDTH
OFF_A = OFF_B + DN_HEADS
OFF_GLU = OFF_A + DN_HEADS
R_QKV = POOL_WIDTH
R_Z = R_QKV + 3 * DN_WIDTH
R_GLU = R_Z + DN_WIDTH
R_BA = R_GLU + 2 * CONF_WIDTH
R_END = R_BA + 128

LANES = 128
POOL_HIST = 32
CONF_HIST = 32
QKV_HIST = 8
GDN_CHUNK = 256
SOLVE_BASE = 16
VMEM_LIMIT = 56 * 1024 * 1024


def _cparams(sem):
    return pltpu.CompilerParams(dimension_semantics=sem, vmem_limit_bytes=VMEM_LIMIT)


def _dot(a, b):
    return jnp.dot(a.astype(BF16), b.astype(BF16), preferred_element_type=F32)


def _dot_nt(a, b):
    return lax.dot_general(a.astype(BF16), b.astype(BF16), (((1,), (1,)), ((), ())),
                           preferred_element_type=F32)


def _dot_tn(a, b):
    return lax.dot_general(a.astype(BF16), b.astype(BF16), (((0,), (0,)), ((), ())),
                           preferred_element_type=F32)


def _split_dot(x, m):
    hi = x.astype(BF16)
    lo = (x - hi.astype(F32)).astype(BF16)
    return (jnp.dot(hi, m, preferred_element_type=F32)
            + jnp.dot(lo, m, preferred_element_type=F32))


def _sigmoid(x):
    return 1.0 / (1.0 + jnp.exp(-x))


def _silu(x):
    return x * _sigmoid(x)


def _softplus(x):
    return jnp.maximum(x, 0.0) + jnp.log1p(jnp.exp(-jnp.abs(x)))


def _mod_rmsnorm(x, g, scale, shift):
    y = x * lax.rsqrt(jnp.mean(x * x, axis=-1, keepdims=True) + EPS)
    return (y * g) * (1.0 + scale) + shift


def _ada_kernel(c_ref, w_ref, b_ref, o_ref):
    o_ref[...] = _dot(_silu(c_ref[...]), w_ref[...]) + b_ref[...]


def _ada(c_all, w_ada, b_ada):
    depth, d, n = w_ada.shape
    rows = c_all.shape[0]
    tn = 1536
    return pl.pallas_call(
        _ada_kernel,
        grid=(depth, n // tn),
        in_specs=[pl.BlockSpec((rows, d), lambda l, j: (0, 0)),
                  pl.BlockSpec((None, d, tn), lambda l, j: (l, 0, j)),
                  pl.BlockSpec((None, 1, tn), lambda l, j: (l, 0, j))],
        out_specs=pl.BlockSpec((None, rows, tn), lambda l, j: (l, 0, j)),
        out_shape=jax.ShapeDtypeStruct((depth, rows, n), F32),
        compiler_params=_cparams(("arbitrary", "arbitrary")),
        name="ada_mod",
    )(c_all, w_ada, b_ada.reshape(depth, 1, n))


def _inproj_kernel(x_ref, shift_ref, scale_ref, g_ref, w_ref, alog_ref, dtb_ref,
                   pool_o, qkv_o, z_o, glu_o, bg_o):
    h = _mod_rmsnorm(x_ref[...], g_ref[...], scale_ref[...], shift_ref[...]).astype(BF16)

    def seg(lo, hi):
        return jnp.dot(h, w_ref[:, lo:hi], preferred_element_type=F32)

    pool_o[...] = seg(0, R_QKV)
    qkv_o[...] = seg(R_QKV, R_Z)
    z_o[...] = seg(R_Z, R_GLU)
    gl = seg(R_GLU, R_BA)
    glu_o[...] = gl[:, :CONF_WIDTH] * _sigmoid(gl[:, CONF_WIDTH:])
    ba = seg(R_BA, R_END)
    lane = lax.broadcasted_iota(jnp.int32, ba.shape, 1)
    beta = _sigmoid(ba)
    g = -jnp.exp(alog_ref[...]) * _softplus(ba + dtb_ref[...])
    bg_o[...] = jnp.where(lane < DN_HEADS, beta, g)


def _mod_spec(mode, chunk, tm, rows_per_batch):
    if mode == "batch":
        per = rows_per_batch // tm
        return pl.BlockSpec((None, 1, D_MODEL), lambda i: ((i // per) * 6 + chunk, 0, 0))
    return pl.BlockSpec((tm, D_MODEL), lambda i: (i, chunk))


def _inproj(x, mod, mode, layer, rows_per_batch, g1, w_in_r, alog_row, dtb_row, tm):
    n = x.shape[0]
    row = lambda w: pl.BlockSpec((tm, w), lambda i: (i, 0))
    const3 = lambda a, b: pl.BlockSpec((None, a, b), lambda i: (layer, 0, 0))
    outs = [POOL_WIDTH, 3 * DN_WIDTH, DN_WIDTH, CONF_WIDTH, LANES]
    return pl.pallas_call(
        _inproj_kernel,
        grid=(n // tm,),
        in_specs=[row(D_MODEL),
                  _mod_spec(mode,0, tm, rows_per_batch),
                  _mod_spec(mode,1, tm, rows_per_batch),
                  const3(1, D_MODEL),
                  pl.BlockSpec((None, D_MODEL, R_END), lambda i: (layer, 0, 0),
                               pipeline_mode=pl.Buffered(1)),
                  const3(1, LANES), const3(1, LANES)],
        out_specs=[row(w) for w in outs],
        out_shape=[jax.ShapeDtypeStruct((n, w), F32) for w in outs],
        compiler_params=_cparams(("parallel",)),
        name="in_proj",
    )(x, mod, mod, g1, w_in_r, alog_row, dtb_row)


def _pool_delta(win, u, pos, half):
    w_lo, w_hi = POOL_WINDOWS[2 * half], POOL_WINDOWS[2 * half + 1]
    a = win(0)
    for j in range(1, w_lo):
        a = a + win(j)
    b = win(w_lo)
    for j in range(w_lo + 1, w_hi):
        b = b + win(j)
    lane = lax.broadcasted_iota(jnp.int32, u.shape, 1)
    upper = lane >= LANES // 2
    s = a + jnp.where(upper, b, 0.0)
    cnt = jnp.where(upper, jnp.minimum(pos + 1, w_hi), jnp.minimum(pos + 1, w_lo)).astype(F32)
    return s / cnt - u


def _conf_post(dc, avg, lng, lnb, cpw):
    mu = _split_dot(dc, avg)
    xc = dc - mu
    var = _split_dot(xc * xc, avg)
    dn = xc * lax.rsqrt(var + EPS) * lng + lnb
    return _dot(_silu(dn), cpw)


def _poolconf_kernel(u_ref, glu_ref, pw_ref, ps_ref, cw_ref, cb_ref, lng_ref, lnb_ref, cpw_ref,
                     avg_ref, ya_o, yc_o, epool, econf, pa, pb, erot, *, tt, sub):
    t = pl.program_id(1)
    n = POOL_HIST + tt

    @pl.when(t == 0)
    def _():
        epool[0:POOL_HIST, :] = jnp.zeros((POOL_HIST, POOL_WIDTH), F32)
        econf[0:CONF_HIST, :] = jnp.zeros((CONF_HIST, CONF_WIDTH), F32)

    @pl.when(t > 0)
    def _():
        epool[0:POOL_HIST, :] = epool[tt:tt + POOL_HIST, :]
        econf[0:CONF_HIST, :] = econf[tt:tt + CONF_HIST, :]

    epool[POOL_HIST:, :] = u_ref[...]
    econf[CONF_HIST:, :] = glu_ref[...]

    pos = t * tt + lax.broadcasted_iota(jnp.int32, (tt, 1), 0)
    upper = lax.broadcasted_iota(jnp.int32, (tt, LANES), 1) >= LANES // 2
    halves = []
    for half in range(2):
        cols = slice(half * LANES, (half + 1) * LANES)
        w_lo, w_hi = POOL_WINDOWS[2 * half], POOL_WINDOWS[2 * half + 1]
        pa[8:n, :] = epool[8:n, cols] + epool[7:n - 1, cols]
        pb[16:n, :] = pa[16:n, :] + pa[14:n - 2, :]
        if half == 0:
            s_lo, s_hi = pa[POOL_HIST:n, :], pb[POOL_HIST:n, :]
        else:
            pa[24:n, :] = pb[24:n, :] + pb[20:n - 4, :]
            s_lo = pa[POOL_HIST:n, :]
            s_hi = s_lo + pa[POOL_HIST - 8:n - 8, :]
        cnt = jnp.where(upper, jnp.minimum(pos + 1, w_hi), jnp.minimum(pos + 1, w_lo)).astype(F32)
        halves.append(jnp.where(upper, s_hi, s_lo) / cnt - epool[POOL_HIST:n, cols])
    d = jnp.concatenate(halves, axis=1)
    ya_o[...] = _dot(d, pw_ref[...]) * ps_ref[...]

    m = CONF_HIST + tt - 8
    for r in range(1, 8):
        erot[r - 1, 0:m, :] = econf[r:r + m, :]
    first = CONF_HIST - (CONF_K - 1)
    dc_blocks = []
    for r0 in range(0, tt, sub):
        acc = cb_ref[...]
        for j in range(CONF_K):
            a8, r = divmod(first + j, 8)
            lo = 8 * a8 + r0
            rows = econf[lo:lo + sub, :] if r == 0 else erot[r - 1, lo:lo + sub, :]
            acc = acc + cw_ref[j:j + 1, :] * rows
        dc_blocks.append(acc)
    dc = jnp.concatenate(dc_blocks, axis=0)
    yc_o[...] = _conf_post(dc, avg_ref[...], lng_ref[...], lnb_ref[...], cpw_ref[...])


def _poolconf(u, glu, layer, batch, seq, pw_bd, pscale, cw, cb, lng, lnb, cpw, avg, tt):
    per = seq // tt
    row = pl.BlockSpec((tt, POOL_WIDTH), lambda b, t: (b * per + t, 0))
    const3 = lambda a, b_: pl.BlockSpec((None, a, b_), lambda b, t: (layer, 0, 0))
    return pl.pallas_call(
        functools.partial(_poolconf_kernel, tt=tt, sub=64),
        grid=(batch, per),
        in_specs=[row, row,
                  const3(POOL_WIDTH, POOL_WIDTH), const3(1, POOL_WIDTH),
                  const3(CONF_K, CONF_WIDTH), const3(1, CONF_WIDTH),
                  const3(1, CONF_WIDTH), const3(1, CONF_WIDTH),
                  const3(CONF_WIDTH, CONF_WIDTH),
                  pl.BlockSpec((CONF_WIDTH, CONF_WIDTH), lambda b, t: (0, 0))],
        out_specs=[row, row],
        out_shape=[jax.ShapeDtypeStruct(u.shape, F32), jax.ShapeDtypeStruct(glu.shape, F32)],
        scratch_shapes=[pltpu.VMEM((POOL_HIST + tt, POOL_WIDTH), F32),
                        pltpu.VMEM((CONF_HIST + tt, CONF_WIDTH), F32),
                        pltpu.VMEM((POOL_HIST + tt, LANES), F32),
                        pltpu.VMEM((POOL_HIST + tt, LANES), F32),
                        pltpu.VMEM((7, CONF_HIST + tt, CONF_WIDTH), F32)],
        compiler_params=_cparams(("parallel", "arbitrary")),
        name="pool_conf",
    )(u, glu, pw_bd, pscale, cw, cb, lng, lnb, cpw, avg)


def _poolconf_step_kernel(u_ref, sp_ref, glu_ref, sc_ref, pw_ref, ps_ref, cw_ref, cb_ref, lng_ref,
                          lnb_ref, cpw_ref, avg_ref, ya_o, yc_o, np_o, nc_o, *, pos0):
    u = u_ref[...]
    rows = u.shape[0]
    pos = jnp.full((rows, 1), pos0, jnp.int32)
    halves = []
    for half in range(2):
        def win(j, half=half):
            if j == 0:
                return u[:, half * LANES:(half + 1) * LANES]
            lo = (POOL_BUF - j) * POOL_WIDTH + half * LANES
            return sp_ref[:, lo:lo + LANES]
        halves.append(_pool_delta(win, win(0), pos, half))
    d = jnp.concatenate(halves, axis=1)
    ya_o[...] = _dot(d, pw_ref[...]) * ps_ref[...]
    np_o[:, 0:(POOL_BUF - 1) * POOL_WIDTH] = sp_ref[:, POOL_WIDTH:]
    np_o[:, (POOL_BUF - 1) * POOL_WIDTH:] = u

    glu = glu_ref[...]
    acc = cb_ref[...] + cw_ref[CONF_K - 1:CONF_K, :] * glu
    for j in range(CONF_K - 1):
        acc = acc + cw_ref[j:j + 1, :] * sc_ref[:, j * CONF_WIDTH:(j + 1) * CONF_WIDTH]
    yc_o[...] = _conf_post(acc, avg_ref[...], lng_ref[...], lnb_ref[...], cpw_ref[...])
    nc_o[:, 0:(CONF_K - 2) * CONF_WIDTH] = sc_ref[:, CONF_WIDTH:]
    nc_o[:, (CONF_K - 2) * CONF_WIDTH:] = glu


def _poolconf_step(u, sp, glu, sc, layer, pw_bd, pscale, cw, cb, lng, lnb, cpw, avg, bt):
    n = u.shape[0]
    row = lambda w: pl.BlockSpec((bt, w), lambda i: (i, 0))
    const3 = lambda a, b_: pl.BlockSpec((None, a, b_), lambda i: (layer, 0, 0))
    widths = [POOL_WIDTH, CONF_WIDTH, sp.shape[1], sc.shape[1]]
    return pl.pallas_call(
        functools.partial(_poolconf_step_kernel, pos0=PAST_LEN),
        grid=(n // bt,),
        in_specs=[row(POOL_WIDTH), row(sp.shape[1]), row(CONF_WIDTH), row(sc.shape[1]),
                  const3(POOL_WIDTH, POOL_WIDTH), const3(1, POOL_WIDTH),
                  const3(CONF_K, CONF_WIDTH), const3(1, CONF_WIDTH),
                  const3(1, CONF_WIDTH), const3(1, CONF_WIDTH),
                  const3(CONF_WIDTH, CONF_WIDTH),
                  pl.BlockSpec((CONF_WIDTH, CONF_WIDTH), lambda i: (0, 0))],
        out_specs=[row(w) for w in widths],
        out_shape=[jax.ShapeDtypeStruct((n, w), F32) for w in widths],
        compiler_params=_cparams(("parallel",)),
        name="pool_conf_step",
    )(u, sp, glu, sc, pw_bd, pscale, cw, cb, lng, lnb, cpw, avg)


def _l2norm(x):
    return x * lax.rsqrt(jnp.sum(x * x, axis=-1, keepdims=True) + EPS)


def _gated_norm(o, ng, z):
    return o * lax.rsqrt(jnp.mean(o * o, axis=-1, keepdims=True) + EPS) * ng * _silu(z)


def _tri_masks(c):
    ri = lax.broadcasted_iota(jnp.int32, (c, c), 0)
    ci = lax.broadcasted_iota(jnp.int32, (c, c), 1)
    masks = [ri // SOLVE_BASE == ci // SOLVE_BASE]
    size = SOLVE_BASE
    while size < c:
        masks.append((ri // (2 * size) == ci // (2 * size)) & (ri // size != ci // size))
        size *= 2
    return masks


def _tri_solve(bs, rhss, c, masks):
    ns = [jnp.where(masks[0], b, 0.0) for b in bs]
    ps = [_dot(n, n) for n in ns]
    power = 2
    while power < SOLVE_BASE:
        if 2 * power < SOLVE_BASE:
            sts = [_dot(jnp.concatenate([n, p], axis=0), p) for n, p in zip(ns, ps)]
            ns = [n + p + st[:c] for n, p, st in zip(ns, ps, sts)]
            ps = [st[c:] for st in sts]
        else:
            ns = [n + p + _dot(n, p) for n, p in zip(ns, ps)]
        power *= 2
    for m in masks[1:]:
        ls = [jnp.where(m, b, 0.0) for b in bs]
        tls = [l + _dot(n, l) for n, l in zip(ns, ls)]
        ns = [n + tl + _dot(tl, n) for n, tl in zip(ns, tls)]
    return [rhs + _dot(n, rhs) for n, rhs in zip(ns, rhss)]


def _gdn_kernel(qkv_ref, z_ref, bg_ref, cw_ref, ng_ref, y_o, s_o, eq, s_scr, *, tt):
    t = pl.program_id(1)
    width = 3 * DN_WIDTH
    dk = DN_HEAD_DIM

    @pl.when(t == 0)
    def _():
        eq[0:QKV_HIST, :] = jnp.zeros((QKV_HIST, width), F32)
        s_scr[...] = jnp.zeros(s_scr.shape, F32)

    @pl.when(t > 0)
    def _():
        eq[0:QKV_HIST, :] = eq[tt:tt + QKV_HIST, :]

    eq[QKV_HIST:, :] = qkv_ref[...]

    def conv_cols(lo):
        base = QKV_HIST - (DN_CONV - 1)
        acc = cw_ref[0:1, lo:lo + LANES] * eq[base:base + tt, lo:lo + LANES]
        for j in range(1, DN_CONV):
            acc = acc + cw_ref[j:j + 1, lo:lo + LANES] * eq[base + j:base + j + tt, lo:lo + LANES]
        return _silu(acc)

    bg = bg_ref[...]
    ri = lax.broadcasted_iota(jnp.int32, (tt, tt), 0)
    ci = lax.broadcasted_iota(jnp.int32, (tt, tt), 1)
    incl = ri >= ci
    strict = ri > ci
    ltri = jnp.where(incl, 1.0, 0.0).astype(BF16)
    g_hi = bg.astype(BF16)
    g_r = bg - g_hi.astype(F32)
    g_mid = g_r.astype(BF16)
    g_lo = (g_r - g_mid.astype(F32)).astype(BF16)
    gc = (jnp.dot(ltri, g_hi, preferred_element_type=F32)
          + jnp.dot(ltri, g_mid, preferred_element_type=F32)
          + jnp.dot(ltri, g_lo, preferred_element_type=F32))
    gt = gc.T
    masks = _tri_masks(tt)

    heads = range(DN_HEADS)
    q = [_l2norm(conv_cols(h * dk)) * (dk ** -0.5) for h in heads]
    k = [_l2norm(conv_cols(DN_WIDTH + h * dk)) for h in heads]
    v = [conv_cols(2 * DN_WIDTH + h * dk) for h in heads]
    beta = [bg[:, h:h + 1] for h in heads]
    gcol = [gc[:, DN_HEADS + h:DN_HEADS + h + 1] for h in heads]
    grow = [gt[DN_HEADS + h:DN_HEADS + h + 1, :] for h in heads]
    eg = [jnp.exp(g) for g in gcol]
    kq = [_dot_nt(jnp.concatenate([k[h], q[h]], axis=0), k[h]) for h in heads]
    decay = [jnp.where(incl, jnp.exp(jnp.where(incl, gcol[h] - grow[h], 0.0)), 0.0) for h in heads]
    bmat = [jnp.where(strict, -(beta[h] * kq[h][:tt] * decay[h]), 0.0) for h in heads]
    qk = [kq[h][tt:] * decay[h] for h in heads]
    rhs = [jnp.concatenate([v[h] * beta[h], k[h] * (beta[h] * eg[h])], axis=1) for h in heads]
    sol = _tri_solve(bmat, rhs, tt, masks)
    g_last = [g[tt - 1:tt] for g in gcol]
    k_tail = [k[h] * jnp.exp(g_last[h] - gcol[h]) for h in heads]
    s = [s_scr[h] for h in heads]
    ws = [_dot(jnp.concatenate([sol[h][:, dk:], q[h] * eg[h]], axis=0), s[h]) for h in heads]
    u = [sol[h][:, :dk] - ws[h][:tt] for h in heads]
    o = [ws[h][tt:] + _dot(qk[h], u[h]) for h in heads]
    for h in heads:
        s_scr[h] = s[h] * jnp.exp(g_last[h]) + _dot_tn(k_tail[h], u[h])
    y = [_gated_norm(o[h], ng_ref[...], z_ref[:, h * dk:(h + 1) * dk]) for h in heads]
    y_o[...] = jnp.concatenate(y, axis=1)

    @pl.when(t == pl.num_programs(1) - 1)
    def _():
        s_o[...] = s_scr[...]


def _gdn(qkv, z, bg, layer, batch, seq, cw, ng, tt):
    per = seq // tt
    row = lambda w: pl.BlockSpec((tt, w), lambda b, t: (b * per + t, 0))
    const3 = lambda a, b_: pl.BlockSpec((None, a, b_), lambda b, t: (layer, 0, 0))
    state = (DN_HEADS, DN_HEAD_DIM, DN_HEAD_DIM)
    return pl.pallas_call(
        functools.partial(_gdn_kernel, tt=tt),
        grid=(batch, per),
        in_specs=[row(3 * DN_WIDTH), row(DN_WIDTH), row(LANES),
                  const3(DN_CONV, 3 * DN_WIDTH), const3(1, DN_HEAD_DIM)],
        out_specs=[row(DN_WIDTH),
                   pl.BlockSpec((None,) + state, lambda b, t: (b, 0, 0, 0))],
        out_shape=[jax.ShapeDtypeStruct(z.shape, F32),
                   jax.ShapeDtypeStruct((batch,) + state, F32)],
        scratch_shapes=[pltpu.VMEM((QKV_HIST + tt, 3 * DN_WIDTH), F32),
                        pltpu.VMEM(state, F32)],
        compiler_params=_cparams(("parallel", "arbitrary")),
        name="gated_delta",
    )(qkv, z, bg, cw, ng)


def _gdn_step_kernel(qkv_ref, sq_ref, z_ref, bg_ref, s_ref, cw_ref, ng_ref, y_o, nq_o, s_o):
    width = 3 * DN_WIDTH
    qkv = qkv_ref[...]
    bt = qkv.shape[0]
    acc = cw_ref[DN_CONV - 1:DN_CONV, :] * qkv
    for j in range(DN_CONV - 1):
        acc = acc + cw_ref[j:j + 1, :] * sq_ref[:, j * width:(j + 1) * width]
    act = _silu(acc)
    nq_o[:, 0:(DN_CONV - 2) * width] = sq_ref[:, width:]
    nq_o[:, (DN_CONV - 2) * width:] = qkv
    bg = bg_ref[...]
    for h in range(DN_HEADS):
        cols = slice(h * DN_HEAD_DIM, (h + 1) * DN_HEAD_DIM)
        q = _l2norm(act[:, cols]) * (DN_HEAD_DIM ** -0.5)
        k = _l2norm(act[:, DN_WIDTH + h * DN_HEAD_DIM:DN_WIDTH + (h + 1) * DN_HEAD_DIM])
        v = act[:, 2 * DN_WIDTH + h * DN_HEAD_DIM:2 * DN_WIDTH + (h + 1) * DN_HEAD_DIM]
        beta = bg[:, h:h + 1]
        eg = jnp.exp(bg[:, DN_HEADS + h:DN_HEADS + h + 1])
        qk = jnp.sum(q * k, axis=-1, keepdims=True)
        outs = []
        for b in range(bt):
            s = s_ref[b, h]
            k_col = jnp.broadcast_to(k[b:b + 1], (DN_HEAD_DIM, DN_HEAD_DIM)).T
            q_col = jnp.broadcast_to(q[b:b + 1], (DN_HEAD_DIM, DN_HEAD_DIM)).T
            ks = jnp.sum(k_col * s, axis=0, keepdims=True)
            qs = jnp.sum(q_col * s, axis=0, keepdims=True)
            e = eg[b:b + 1]
            u = beta[b:b + 1] * (v[b:b + 1] - e * ks)
            outs.append(e * qs + qk[b:b + 1] * u)
            s_o[b, h] = e * s + k_col * u
        o = jnp.concatenate(outs, axis=0)
        y_o[:, cols] = _gated_norm(o, ng_ref[...], z_ref[:, cols])


def _gdn_step(qkv, sq, z, bg, s, layer, cw, ng, bt):
    n = qkv.shape[0]
    row = lambda w: pl.BlockSpec((bt, w), lambda i: (i, 0))
    const3 = lambda a, b_: pl.BlockSpec((None, a, b_), lambda i: (layer, 0, 0))
    sblk = pl.BlockSpec((bt, DN_HEADS, DN_HEAD_DIM, DN_HEAD_DIM), lambda i: (i, 0, 0, 0))
    return pl.pallas_call(
        _gdn_step_kernel,
        grid=(n // bt,),
        in_specs=[row(3 * DN_WIDTH), row(sq.shape[1]), row(DN_WIDTH), row(LANES), sblk,
                  const3(DN_CONV, 3 * DN_WIDTH), const3(1, DN_HEAD_DIM)],
        out_specs=[row(DN_WIDTH), row(sq.shape[1]), sblk],
        out_shape=[jax.ShapeDtypeStruct(z.shape, F32), jax.ShapeDtypeStruct(sq.shape, F32),
                   jax.ShapeDtypeStruct(s.shape, F32)],
        compiler_params=_cparams(("parallel",)),
        name="gated_delta_step",
    )(qkv, sq, z, bg, s, cw, ng)


def _outproj_kernel(x_ref, ya_ref, yb_ref, yc_ref, gate_ref, w_ref, o_ref):
    a_end, b_end = POOL_WIDTH, POOL_WIDTH + DN_WIDTH
    mix = (_dot(ya_ref[...], w_ref[0:a_end, :]) + _dot(yb_ref[...], w_ref[a_end:b_end, :])
           + _dot(yc_ref[...], w_ref[b_end:, :]))
    o_ref[...] = x_ref[...] + gate_ref[...] * mix


def _outproj(x, ya, yb, yc, mod, mode, layer, rows_per_batch, w_out, tm):
    n = x.shape[0]
    row = lambda w: pl.BlockSpec((tm, w), lambda i: (i, 0))
    return pl.pallas_call(
        _outproj_kernel,
        grid=(n // tm,),
        in_specs=[row(D_MODEL), row(POOL_WIDTH), row(DN_WIDTH), row(CONF_WIDTH),
                  _mod_spec(mode,2, tm, rows_per_batch),
                  pl.BlockSpec((None, D_MODEL, D_MODEL), lambda i: (layer, 0, 0),
                               pipeline_mode=pl.Buffered(1))],
        out_specs=row(D_MODEL),
        out_shape=jax.ShapeDtypeStruct(x.shape, F32),
        compiler_params=_cparams(("parallel",)),
        name="out_proj",
    )(x, ya, yb, yc, mod, w_out)


def _ffn_kernel(x_ref, shift_ref, scale_ref, gate_ref, g_ref, w1_ref, w2_ref, gf_ref, o_ref, *,
                final, hidden_chunk):
    x = x_ref[...]
    h = _mod_rmsnorm(x, g_ref[...], scale_ref[...], shift_ref[...]).astype(BF16)
    acc = jnp.zeros(x.shape, F32)
    for c in range(0, D_FF, hidden_chunk):
        a = jnp.maximum(jnp.dot(h, w1_ref[:, c:c + hidden_chunk], preferred_element_type=F32), 0.0)
        acc = acc + jnp.dot((a * a).astype(BF16), w2_ref[c:c + hidden_chunk, :],
                            preferred_element_type=F32)
    y = x + gate_ref[...] * acc
    if final:
        y = y * lax.rsqrt(jnp.mean(y * y, axis=-1, keepdims=True) + EPS) * gf_ref[...]
    o_ref[...] = y


def _ffn(x, mod, mode, layer, rows_per_batch, g2, w1, w2, g_final, final, tm):
    n = x.shape[0]
    row = pl.BlockSpec((tm, D_MODEL), lambda i: (i, 0))
    return pl.pallas_call(
        functools.partial(_ffn_kernel, final=final, hidden_chunk=1024),
        grid=(n // tm,),
        in_specs=[row,
                  _mod_spec(mode,3, tm, rows_per_batch),
                  _mod_spec(mode,4, tm, rows_per_batch),
                  _mod_spec(mode,5, tm, rows_per_batch),
                  pl.BlockSpec((None, 1, D_MODEL), lambda i: (layer, 0, 0)),
                  pl.BlockSpec((None, D_MODEL, D_FF), lambda i: (layer, 0, 0),
                               pipeline_mode=pl.Buffered(1)),
                  pl.BlockSpec((None, D_FF, D_MODEL), lambda i: (layer, 0, 0),
                               pipeline_mode=pl.Buffered(1)),
                  pl.BlockSpec((1, D_MODEL), lambda i: (0, 0))],
        out_specs=row,
        out_shape=jax.ShapeDtypeStruct(x.shape, F32),
        compiler_params=_cparams(("parallel",)),
        name="ffn",
    )(x, mod, mod, mod, g2, w1, w2, g_final)


def kernel(x_prompt, x_sample, state_pool, state_qkv_conv, state_delta, state_conv, c_prompt, c_sample,
           w_ada, b_ada, g_norm1, g_norm2, w_in, pool_w, pool_scale, qkv_conv_w, a_log, dt_bias,
           dn_norm_g, conf_dw_w, conf_dw_b, conf_ln_g, conf_ln_b, conf_pw_w, w_out, w_ff1, w_ff2, g_final):
    depth = w_in.shape[0]
    bp, seq, d = x_prompt.shape
    bs = x_sample.shape[0]
    assert d == D_MODEL and x_sample.shape[1] == 1

    ba_cols = jnp.pad(w_in[:, :, OFF_B:OFF_GLU], ((0, 0), (0, 0), (0, LANES - 2 * DN_HEADS)))
    w_in_r = jnp.concatenate([w_in[:, :, :OFF_B], w_in[:, :, OFF_GLU:], ba_cols], axis=-1).astype(BF16)
    groups = pool_w.shape[1]
    pw_bd = jnp.einsum("lgcd,gh->lgchd", pool_w, jnp.eye(groups, dtype=pool_w.dtype))
    pw_bd = pw_bd.reshape(depth, POOL_WIDTH, POOL_WIDTH).astype(BF16)
    head_of = jnp.arange(CONF_WIDTH) // (CONF_WIDTH // CONF_HEADS)
    avg = jnp.where(head_of[:, None] == head_of[None, :], CONF_HEADS / CONF_WIDTH, 0.0).astype(BF16)
    lane_pad = lambda a: jnp.pad(a, ((0, 0), (DN_HEADS, LANES - 2 * DN_HEADS)))[:, None, :]
    alog_row, dtb_row = lane_pad(a_log), lane_pad(dt_bias)
    r3 = lambda a: a[:, None, :]
    g1, g2, pscale, ng = r3(g_norm1), r3(g_norm2), r3(pool_scale), r3(dn_norm_g)
    cb, lng, lnb = r3(conf_dw_b), r3(conf_ln_g), r3(conf_ln_b)
    cpw = conf_pw_w.astype(BF16)
    w_out_b, w1_b, w2_b = w_out.astype(BF16), w_ff1.astype(BF16), w_ff2.astype(BF16)
    gf = g_final[None, :]

    mods = _ada(jnp.concatenate([c_prompt, c_sample], axis=0), w_ada, b_ada)
    mods_p = mods[:, :bp].reshape(depth, bp * 6, 1, d)
    mods_s = mods[:, bp:]

    xp = x_prompt.reshape(bp * seq, d)
    xs = x_sample.reshape(bs, d)
    tm_p, tt_mix, tt_gdn, bt_mix, bt_gdn = 512, 256, GDN_CHUNK, min(32, bs), min(8, bs)
    outs = [[] for _ in range(8)]
    for l in range(depth):
        last = l == depth - 1
        mp = mods_p[l]
        u, qkv, z, glu, bg = _inproj(xp, mp, "batch", l, seq, g1, w_in_r, alog_row, dtb_row, tm_p)
        ya, yc = _poolconf(u, glu, l, bp, seq, pw_bd, pscale, conf_dw_w, cb, lng, lnb, cpw, avg, tt_mix)
        yb, s_new = _gdn(qkv, z, bg, l, bp, seq, qkv_conv_w, ng, tt_gdn)
        xp = _outproj(xp, ya, yb, yc, mp, "batch", l, seq, w_out_b, tm_p)
        xp = _ffn(xp, mp, "batch", l, seq, g2, w1_b, w2_b, gf, last, tm_p)
        outs[0].append(u.reshape(bp, seq, -1)[:, seq - POOL_BUF:])
        outs[2].append(qkv.reshape(bp, seq, -1)[:, seq - (DN_CONV - 1):])
        outs[4].append(s_new)
        outs[6].append(glu.reshape(bp, seq, -1)[:, seq - (CONF_K - 1):])
        ms = mods_s[l]
        u, qkv, z, glu, bg = _inproj(xs, ms, "row", l, 1, g1, w_in_r, alog_row, dtb_row, bs)
        ya, yc, n_pool, n_conf = _poolconf_step(
            u, state_pool[l].reshape(bs, -1), glu, state_conv[l].reshape(bs, -1), l,
            pw_bd, pscale, conf_dw_w, cb, lng, lnb, cpw, avg, bt_mix)
        yb, n_qkv, s_new = _gdn_step(qkv, state_qkv_conv[l].reshape(bs, -1), z, bg, state_delta[l], l,
                                     qkv_conv_w, ng, bt_gdn)
        xs = _outproj(xs, ya, yb, yc, ms, "row", l, 1, w_out_b, bs)
        xs = _ffn(xs, ms, "row", l, 1, g2, w1_b, w2_b, gf, last, bs)
        outs[1].append(n_pool.reshape(bs, POOL_BUF, POOL_WIDTH))
        outs[3].append(n_qkv.reshape(bs, DN_CONV - 1, 3 * DN_WIDTH))
        outs[5].append(s_new)
        outs[7].append(n_conf.reshape(bs, CONF_K - 1, CONF_WIDTH))
    stacked = [jnp.stack(o) for o in outs]
    return (xp.reshape(bp, seq, d), xs.reshape(bs, 1, d)) + tuple(stacked)
```

```python
import functools

import jax
import jax.numpy as jnp
from jax import lax
from jax.experimental import pallas as pl
from jax.experimental.pallas import tpu as pltpu

F32 = jnp.float32
BF16 = jnp.bfloat16
EPS = 1e-6

D_MODEL = 1024
POOL_WIDTH = 256
POOL_WINDOWS = (2, 4, 8, 16)
POOL_BUF = 15
DN_WIDTH = 512
DN_HEAD_DIM = 128
DN_HEADS = 4
DN_CONV = 4
CONF_WIDTH = 256
CONF_HEADS = 4
CONF_K = 31
D_FF = 4 * D_MODEL
PAST_LEN = 16384
OFF_QKV = POOL_WIDTH
OFF_Z = OFF_QKV + 3 * DN_WIDTH
OFF_B = OFF_Z + DN_WIDTH
OFF_A = OFF_B + DN_HEADS
OFF_GLU = OFF_A + DN_HEADS
N_IN = OFF_GLU + 2 * CONF_WIDTH

LANES = 128
POOL_HIST = 32
CONF_HIST = 32
QKV_HIST = 8
GDN_CHUNK = 128
GDN_TILE = 512
SOLVE_BASE = 16
VMEM_LIMIT = 56 * 1024 * 1024


def _cparams(sem):
    return pltpu.CompilerParams(dimension_semantics=sem, vmem_limit_bytes=VMEM_LIMIT)


def _dot(a, b):
    return jnp.dot(a.astype(BF16), b.astype(BF16), preferred_element_type=F32)


def _lane_sum(x):
    ones = jnp.ones((x.shape[1], LANES), BF16)
    return jnp.dot(x.astype(BF16), ones, preferred_element_type=F32)


def _split_dot(x, m):
    hi = x.astype(BF16)
    lo = (x - hi.astype(F32)).astype(BF16)
    return (jnp.dot(hi, m, preferred_element_type=F32)
            + jnp.dot(lo, m, preferred_element_type=F32))


def _sigmoid(x):
    return 1.0 / (1.0 + jnp.exp(-x))


def _silu(x):
    return x * _sigmoid(x)


def _softplus(x):
    return jnp.maximum(x, 0.0) + jnp.log1p(jnp.exp(-jnp.abs(x)))


def _mod_rmsnorm(x, g, scale, shift):
    y = x * lax.rsqrt(jnp.mean(x * x, axis=-1, keepdims=True) + EPS)
    return (y * g) * (1.0 + scale) + shift


def _ada_kernel(c_ref, w_ref, b_ref, o_ref):
    o_ref[...] = _dot(_silu(c_ref[...]), w_ref[...]) + b_ref[...]


def _ada(c_all, w_ada, b_ada):
    depth, d, n = w_ada.shape
    rows = c_all.shape[0]
    tn = 1536
    return pl.pallas_call(
        _ada_kernel,
        grid=(depth, n // tn),
        in_specs=[pl.BlockSpec((rows, d), lambda l, j: (0, 0)),
                  pl.BlockSpec((None, d, tn), lambda l, j: (l, 0, j)),
                  pl.BlockSpec((None, 1, tn), lambda l, j: (l, 0, j))],
        out_specs=pl.BlockSpec((None, rows, tn), lambda l, j: (l, 0, j)),
        out_shape=jax.ShapeDtypeStruct((depth, rows, n), F32),
        compiler_params=_cparams(("arbitrary", "arbitrary")),
        name="ada_mod",
    )(c_all, w_ada, b_ada.reshape(depth, 1, n))


def _inproj_kernel(x_ref, shift_ref, scale_ref, g_ref, w_ref, alog_ref, dtb_ref,
                   pool_o, qkv_o, z_o, glu_o, bg_o):
    h = _mod_rmsnorm(x_ref[...], g_ref[...], scale_ref[...], shift_ref[...]).astype(BF16)

    def seg(lo, hi):
        return jnp.dot(h, w_ref[:, lo:hi], preferred_element_type=F32)

    pool_o[...] = seg(0, OFF_QKV)
    qkv_o[...] = seg(OFF_QKV, OFF_Z)
    z_o[...] = seg(OFF_Z, OFF_B)
    tail = seg(OFF_B, N_IN)
    glu_lo = OFF_GLU - OFF_B
    glu_o[...] = (tail[:, glu_lo:glu_lo + CONF_WIDTH]
                  * _sigmoid(tail[:, glu_lo + CONF_WIDTH:glu_lo + 2 * CONF_WIDTH]))
    ba = tail[:, :LANES]
    lane = lax.broadcasted_iota(jnp.int32, ba.shape, 1)
    beta = _sigmoid(ba)
    g = -jnp.exp(alog_ref[...]) * _softplus(ba + dtb_ref[...])
    bg_o[...] = jnp.where(lane < DN_HEADS, beta, g)


def _mod_spec(mode, chunk, tm, rows_per_batch):
    if mode == "batch":
        per = rows_per_batch // tm
        return pl.BlockSpec((None, 1, D_MODEL), lambda i: ((i // per) * 6 + chunk, 0, 0))
    return pl.BlockSpec((tm, D_MODEL), lambda i: (i, chunk))


def _inproj(x, mod, mode, layer, rows_per_batch, g1, w_in_r, alog_row, dtb_row, tm):
    n = x.shape[0]
    row = lambda w: pl.BlockSpec((tm, w), lambda i: (i, 0))
    const3 = lambda a, b: pl.BlockSpec((None, a, b), lambda i: (layer, 0, 0))
    outs = [POOL_WIDTH, 3 * DN_WIDTH, DN_WIDTH, CONF_WIDTH, LANES]
    return pl.pallas_call(
        _inproj_kernel,
        grid=(n // tm,),
        in_specs=[row(D_MODEL),
                  _mod_spec(mode,0, tm, rows_per_batch),
                  _mod_spec(mode,1, tm, rows_per_batch),
                  const3(1, D_MODEL),
                  pl.BlockSpec((None, D_MODEL, N_IN), lambda i: (layer, 0, 0),
                               pipeline_mode=pl.Buffered(1)),
                  const3(1, LANES), const3(1, LANES)],
        out_specs=[row(w) for w in outs],
        out_shape=[jax.ShapeDtypeStruct((n, w), F32) for w in outs],
        compiler_params=_cparams(("parallel",)),
        name="in_proj",
    )(x, mod, mod, g1, w_in_r, alog_row, dtb_row)


def _pool_delta(win, u, pos, half):
    w_lo, w_hi = POOL_WINDOWS[2 * half], POOL_WINDOWS[2 * half + 1]
    a = win(0)
    for j in range(1, w_lo):
        a = a + win(j)
    b = win(w_lo)
    for j in range(w_lo + 1, w_hi):
        b = b + win(j)
    lane = lax.broadcasted_iota(jnp.int32, u.shape, 1)
    upper = lane >= LANES // 2
    s = a + jnp.where(upper, b, 0.0)
    cnt = jnp.where(upper, jnp.minimum(pos + 1, w_hi), jnp.minimum(pos + 1, w_lo)).astype(F32)
    return s / cnt - u


def _conf_post(dc, avg, lng, lnb, cpw):
    mu = _split_dot(dc, avg)
    xc = dc - mu
    var = _split_dot(xc * xc, avg)
    dn = xc * lax.rsqrt(var + EPS) * lng + lnb
    return _dot(_silu(dn), cpw)


def _poolconf_kernel(u_ref, glu_ref, pw_ref, ps_ref, cw_ref, cb_ref, lng_ref, lnb_ref, cpw_ref,
                     avg_ref, ya_o, yc_o, epool, econf, pa, pb, erot, *, tt, sub):
    t = pl.program_id(1)
    n = POOL_HIST + tt

    @pl.when(t == 0)
    def _():
        epool[0:POOL_HIST, :] = jnp.zeros((POOL_HIST, POOL_WIDTH), F32)
        econf[0:CONF_HIST, :] = jnp.zeros((CONF_HIST, CONF_WIDTH), F32)

    @pl.when(t > 0)
    def _():
        epool[0:POOL_HIST, :] = epool[tt:tt + POOL_HIST, :]
        econf[0:CONF_HIST, :] = econf[tt:tt + CONF_HIST, :]

    epool[POOL_HIST:, :] = u_ref[...]
    econf[CONF_HIST:, :] = glu_ref[...]

    pos = t * tt + lax.broadcasted_iota(jnp.int32, (tt, 1), 0)
    upper = lax.broadcasted_iota(jnp.int32, (tt, LANES), 1) >= LANES // 2
    halves = []
    for half in range(2):
        cols = slice(half * LANES, (half + 1) * LANES)
        w_lo, w_hi = POOL_WINDOWS[2 * half], POOL_WINDOWS[2 * half + 1]
        pa[8:n, :] = epool[8:n, cols] + epool[7:n - 1, cols]
        pb[16:n, :] = pa[16:n, :] + pa[14:n - 2, :]
        if half == 0:
            s_lo, s_hi = pa[POOL_HIST:n, :], pb[POOL_HIST:n, :]
        else:
            pa[24:n, :] = pb[24:n, :] + pb[20:n - 4, :]
            s_lo = pa[POOL_HIST:n, :]
            s_hi = s_lo + pa[POOL_HIST - 8:n - 8, :]
        cnt = jnp.where(upper, jnp.minimum(pos + 1, w_hi), jnp.minimum(pos + 1, w_lo)).astype(F32)
        halves.append(jnp.where(upper, s_hi, s_lo) / cnt - epool[POOL_HIST:n, cols])
    d = jnp.concatenate(halves, axis=1)
    ya_o[...] = _dot(d, pw_ref[...]) * ps_ref[...]

    m = CONF_HIST + tt - 8
    for r in range(1, 8):
        erot[r - 1, 0:m, :] = econf[r:r + m, :]
    first = CONF_HIST - (CONF_K - 1)
    dc_blocks = []
    for r0 in range(0, tt, sub):
        acc = cb_ref[...]
        for j in range(CONF_K):
            a8, r = divmod(first + j, 8)
            lo = 8 * a8 + r0
            rows = econf[lo:lo + sub, :] if r == 0 else erot[r - 1, lo:lo + sub, :]
            acc = acc + cw_ref[j:j + 1, :] * rows
        dc_blocks.append(acc)
    dc = jnp.concatenate(dc_blocks, axis=0)
    yc_o[...] = _conf_post(dc, avg_ref[...], lng_ref[...], lnb_ref[...], cpw_ref[...])


def _poolconf(u, glu, layer, batch, seq, pw_bd, pscale, cw, cb, lng, lnb, cpw, avg, tt):
    per = seq // tt
    row = pl.BlockSpec((tt, POOL_WIDTH), lambda b, t: (b * per + t, 0))
    const3 = lambda a, b_: pl.BlockSpec((None, a, b_), lambda b, t: (layer, 0, 0))
    return pl.pallas_call(
        functools.partial(_poolconf_kernel, tt=tt, sub=64),
        grid=(batch, per),
        in_specs=[row, row,
                  const3(POOL_WIDTH, POOL_WIDTH), const3(1, POOL_WIDTH),
                  const3(CONF_K, CONF_WIDTH), const3(1, CONF_WIDTH),
                  const3(1, CONF_WIDTH), const3(1, CONF_WIDTH),
                  const3(CONF_WIDTH, CONF_WIDTH),
                  pl.BlockSpec((CONF_WIDTH, CONF_WIDTH), lambda b, t: (0, 0))],
        out_specs=[row, row],
        out_shape=[jax.ShapeDtypeStruct(u.shape, F32), jax.ShapeDtypeStruct(glu.shape, F32)],
        scratch_shapes=[pltpu.VMEM((POOL_HIST + tt, POOL_WIDTH), F32),
                        pltpu.VMEM((CONF_HIST + tt, CONF_WIDTH), F32),
                        pltpu.VMEM((POOL_HIST + tt, LANES), F32),
                        pltpu.VMEM((POOL_HIST + tt, LANES), F32),
                        pltpu.VMEM((7, CONF_HIST + tt, CONF_WIDTH), F32)],
        compiler_params=_cparams(("parallel", "arbitrary")),
        name="pool_conf",
    )(u, glu, pw_bd, pscale, cw, cb, lng, lnb, cpw, avg)


def _poolconf_step_kernel(u_ref, sp_ref, glu_ref, sc_ref, pw_ref, ps_ref, cw_ref, cb_ref, lng_ref,
                          lnb_ref, cpw_ref, avg_ref, ya_o, yc_o, np_o, nc_o, *, pos0):
    u = u_ref[...]
    rows = u.shape[0]
    pos = jnp.full((rows, 1), pos0, jnp.int32)
    halves = []
    for half in range(2):
        def win(j, half=half):
            if j == 0:
                return u[:, half * LANES:(half + 1) * LANES]
            lo = (POOL_BUF - j) * POOL_WIDTH + half * LANES
            return sp_ref[:, lo:lo + LANES]
        halves.append(_pool_delta(win, win(0), pos, half))
    d = jnp.concatenate(halves, axis=1)
    ya_o[...] = _dot(d, pw_ref[...]) * ps_ref[...]
    np_o[:, 0:(POOL_BUF - 1) * POOL_WIDTH] = sp_ref[:, POOL_WIDTH:]
    np_o[:, (POOL_BUF - 1) * POOL_WIDTH:] = u

    glu = glu_ref[...]
    acc = cb_ref[...] + cw_ref[CONF_K - 1:CONF_K, :] * glu
    for j in range(CONF_K - 1):
        acc = acc + cw_ref[j:j + 1, :] * sc_ref[:, j * CONF_WIDTH:(j + 1) * CONF_WIDTH]
    yc_o[...] = _conf_post(acc, avg_ref[...], lng_ref[...], lnb_ref[...], cpw_ref[...])
    nc_o[:, 0:(CONF_K - 2) * CONF_WIDTH] = sc_ref[:, CONF_WIDTH:]
    nc_o[:, (CONF_K - 2) * CONF_WIDTH:] = glu


def _poolconf_step(u, sp, glu, sc, layer, pw_bd, pscale, cw, cb, lng, lnb, cpw, avg, bt):
    n = u.shape[0]
    row = lambda w: pl.BlockSpec((bt, w), lambda i: (i, 0))
    const3 = lambda a, b_: pl.BlockSpec((None, a, b_), lambda i: (layer, 0, 0))
    widths = [POOL_WIDTH, CONF_WIDTH, sp.shape[1], sc.shape[1]]
    return pl.pallas_call(
        functools.partial(_poolconf_step_kernel, pos0=PAST_LEN),
        grid=(n // bt,),
        in_specs=[row(POOL_WIDTH), row(sp.shape[1]), row(CONF_WIDTH), row(sc.shape[1]),
                  const3(POOL_WIDTH, POOL_WIDTH), const3(1, POOL_WIDTH),
                  const3(CONF_K, CONF_WIDTH), const3(1, CONF_WIDTH),
                  const3(1, CONF_WIDTH), const3(1, CONF_WIDTH),
                  const3(CONF_WIDTH, CONF_WIDTH),
                  pl.BlockSpec((CONF_WIDTH, CONF_WIDTH), lambda i: (0, 0))],
        out_specs=[row(w) for w in widths],
        out_shape=[jax.ShapeDtypeStruct((n, w), F32) for w in widths],
        compiler_params=_cparams(("parallel",)),
        name="pool_conf_step",
    )(u, sp, glu, sc, pw_bd, pscale, cw, cb, lng, lnb, cpw, avg)


def _l2norm(x):
    return x * lax.rsqrt(jnp.sum(x * x, axis=-1, keepdims=True) + EPS)


def _gated_norm(o, ng, z):
    return o * lax.rsqrt(jnp.mean(o * o, axis=-1, keepdims=True) + EPS) * ng * _silu(z)


def _tri_masks(c):
    ri = lax.broadcasted_iota(jnp.int32, (c, c), 0)
    ci = lax.broadcasted_iota(jnp.int32, (c, c), 1)
    masks = [ri // SOLVE_BASE == ci // SOLVE_BASE]
    size = SOLVE_BASE
    while size < c:
        masks.append((ri // (2 * size) == ci // (2 * size)) & (ri // size != ci // size))
        size *= 2
    return masks


def _tri_solve(bs, rhss, c, masks):
    ns = [jnp.where(masks[0], b, 0.0) for b in bs]
    ps = [_dot(n, n) for n in ns]
    power = 2
    while power < SOLVE_BASE:
        if 2 * power < SOLVE_BASE:
            sts = [_dot(jnp.concatenate([n, p], axis=0), p) for n, p in zip(ns, ps)]
            ns = [n + p + st[:c] for n, p, st in zip(ns, ps, sts)]
            ps = [st[c:] for st in sts]
        else:
            ns = [n + p + _dot(n, p) for n, p in zip(ns, ps)]
        power *= 2
    for m in masks[1:]:
        ls = [jnp.where(m, b, 0.0) for b in bs]
        tls = [l + _dot(n, l) for n, l in zip(ns, ls)]
        ns = [n + tl + _dot(tl, n) for n, tl in zip(ns, tls)]
    return [rhs + _dot(n, rhs) for n, rhs in zip(ns, rhss)]


def _gdn_kernel(qkv_ref, z_ref, bg_ref, cw_ref, ng_ref, ltri_ref, selb_ref, selg_ref, y_o, s_o,
                eq, eqs, s_scr, *, tt, chunk):
    t = pl.program_id(1)
    width = 3 * DN_WIDTH
    dk = DN_HEAD_DIM

    @pl.when(t == 0)
    def _():
        eq[0:QKV_HIST, :] = jnp.zeros((QKV_HIST, width), F32)
        s_scr[...] = jnp.zeros(s_scr.shape, F32)

    @pl.when(t > 0)
    def _():
        eq[0:QKV_HIST, :] = eq[tt:tt + QKV_HIST, :]

    eq[QKV_HIST:, :] = qkv_ref[...]

    for r in range(1, DN_CONV):
        eqs[r - 1] = eq[QKV_HIST - r:QKV_HIST - r + tt, :]

    def conv_cols(lo):
        acc = cw_ref[DN_CONV - 1:DN_CONV, lo:lo + LANES] * eq[QKV_HIST:QKV_HIST + tt, lo:lo + LANES]
        for r in range(1, DN_CONV):
            acc = acc + cw_ref[DN_CONV - 1 - r:DN_CONV - r, lo:lo + LANES] * eqs[r - 1, :, lo:lo + LANES]
        return _silu(acc)

    def split3(x):
        hi = x.astype(BF16)
        r = x - hi.astype(F32)
        mid = r.astype(BF16)
        return hi, mid, (r - mid.astype(F32)).astype(BF16)

    def dot3(a, pieces, left):
        return sum(jnp.dot(a, p, preferred_element_type=F32) if left else
                   jnp.dot(p, a, preferred_element_type=F32) for p in pieces)

    bg3 = split3(bg_ref[...])
    gc = dot3(ltri_ref[...], bg3, True)
    gt = gc.T
    beta_d = dot3(selb_ref[...], bg3, False)
    g_d = dot3(selg_ref[...], split3(gc), False)
    eg_d = jnp.exp(g_d)
    rc = lax.broadcasted_iota(jnp.int32, (chunk, chunk), 0)
    cc = lax.broadcasted_iota(jnp.int32, (chunk, chunk), 1)
    incl = rc >= cc
    strict = rc > cc
    masks = _tri_masks(chunk)

    heads = range(DN_HEADS)
    chunks = range(tt // chunk)
    rows = [slice(c * chunk, (c + 1) * chunk) for c in chunks]
    def l2n(x):
        return x * lax.rsqrt(_lane_sum(x * x) + EPS)

    q = [l2n(conv_cols(h * dk)) * (dk ** -0.5) for h in heads]
    k = [l2n(conv_cols(DN_WIDTH + h * dk)) for h in heads]
    v = [conv_cols(2 * DN_WIDTH + h * dk) for h in heads]
    cols = [slice(h * dk, (h + 1) * dk) for h in heads]
    beta = [beta_d[:, cols[h]] for h in heads]
    gcol = [g_d[:, cols[h]] for h in heads]
    eg = [eg_d[:, cols[h]] for h in heads]
    grow = [gt[DN_HEADS + h:DN_HEADS + h + 1, :] for h in heads]
    qg = [q[h] * eg[h] for h in heads]
    rhs_all = [jnp.concatenate([v[h] * beta[h], k[h] * (beta[h] * eg[h])], axis=1) for h in heads]

    pairs = [(h, c) for h in heads for c in chunks]
    kq = [_dot(jnp.concatenate([k[h][rows[c]], q[h][rows[c]]], axis=0), k[h][rows[c]].T)
          for h, c in pairs]
    decay = [jnp.where(incl, jnp.exp(jnp.where(incl, gcol[h][rows[c]] - grow[h][:, rows[c]], 0.0)), 0.0)
             for h, c in pairs]
    bmat = [jnp.where(strict, -(beta[h][rows[c]] * kq[i][:chunk] * decay[i]), 0.0)
            for i, (h, c) in enumerate(pairs)]
    qk = [kq[i][chunk:] * decay[i] for i in range(len(pairs))]
    sol = _tri_solve(bmat, [rhs_all[h][rows[c]] for h, c in pairs], chunk, masks)
    g_last = [gcol[h][(c + 1) * chunk - 1:(c + 1) * chunk] for h, c in pairs]
    k_tail = [k[h][rows[c]] * jnp.exp(g_last[i] - gcol[h][rows[c]]) for i, (h, c) in enumerate(pairs)]

    s = [s_scr[h] for h in heads]
    o = [[] for _ in heads]
    for c in chunks:
        idx = [pairs.index((h, c)) for h in heads]
        ws = [_dot(jnp.concatenate([sol[idx[h]][:, dk:], qg[h][rows[c]]], axis=0), s[h]) for h in heads]
        u = [sol[idx[h]][:, :dk] - ws[h][:chunk] for h in heads]
        for h in heads:
            o[h].append(ws[h][chunk:] + _dot(qk[idx[h]], u[h]))
        s = [s[h] * jnp.exp(g_last[idx[h]]) + _dot(k_tail[idx[h]].T, u[h]) for h in heads]
    for h in heads:
        s_scr[h] = s[h]
    y = []
    for h in heads:
        oh = jnp.concatenate(o[h], axis=0)
        rms = lax.rsqrt(_lane_sum(oh * oh) * (1.0 / dk) + EPS)
        y.append(oh * rms * ng_ref[...] * _silu(z_ref[:, h * dk:(h + 1) * dk]))
    y_o[...] = jnp.concatenate(y, axis=1)

    @pl.when(t == pl.num_programs(1) - 1)
    def _():
        s_o[...] = s_scr[...]


def _gdn(qkv, z, bg, layer, batch, seq, cw, ng, consts, tt):
    assert GDN_CHUNK == LANES and DN_HEAD_DIM == LANES
    per = seq // tt
    row = lambda w: pl.BlockSpec((tt, w), lambda b, t: (b * per + t, 0))
    const3 = lambda a, b_: pl.BlockSpec((None, a, b_), lambda b, t: (layer, 0, 0))
    state = (DN_HEADS, DN_HEAD_DIM, DN_HEAD_DIM)
    return pl.pallas_call(
        functools.partial(_gdn_kernel, tt=tt, chunk=GDN_CHUNK),
        grid=(batch, per),
        in_specs=[row(3 * DN_WIDTH), row(DN_WIDTH), row(LANES),
                  const3(DN_CONV, 3 * DN_WIDTH), const3(1, DN_HEAD_DIM),
                  pl.BlockSpec((tt, tt), lambda b, t: (0, 0)),
                  pl.BlockSpec((LANES, DN_WIDTH), lambda b, t: (0, 0)),
                  pl.BlockSpec((LANES, DN_WIDTH), lambda b, t: (0, 0))],
        out_specs=[row(DN_WIDTH),
                   pl.BlockSpec((None,) + state, lambda b, t: (b, 0, 0, 0))],
        out_shape=[jax.ShapeDtypeStruct(z.shape, F32),
                   jax.ShapeDtypeStruct((batch,) + state, F32)],
        scratch_shapes=[pltpu.VMEM((QKV_HIST + tt, 3 * DN_WIDTH), F32),
                        pltpu.VMEM((DN_CONV - 1, tt, 3 * DN_WIDTH), F32),
                        pltpu.VMEM(state, F32)],
        compiler_params=_cparams(("parallel", "arbitrary")),
        name="gated_delta",
    )(qkv, z, bg, cw, ng, *consts)


def _gdn_step_kernel(qkv_ref, sq_ref, z_ref, bg_ref, s_ref, cw_ref, ng_ref, y_o, nq_o, s_o):
    width = 3 * DN_WIDTH
    qkv = qkv_ref[...]
    bt = qkv.shape[0]
    acc = cw_ref[DN_CONV - 1:DN_CONV, :] * qkv
    for j in range(DN_CONV - 1):
        acc = acc + cw_ref[j:j + 1, :] * sq_ref[:, j * width:(j + 1) * width]
    act = _silu(acc)
    nq_o[:, 0:(DN_CONV - 2) * width] = sq_ref[:, width:]
    nq_o[:, (DN_CONV - 2) * width:] = qkv
    bg = bg_ref[...]
    for h in range(DN_HEADS):
        cols = slice(h * DN_HEAD_DIM, (h + 1) * DN_HEAD_DIM)
        q = _l2norm(act[:, cols]) * (DN_HEAD_DIM ** -0.5)
        k = _l2norm(act[:, DN_WIDTH + h * DN_HEAD_DIM:DN_WIDTH + (h + 1) * DN_HEAD_DIM])
        v = act[:, 2 * DN_WIDTH + h * DN_HEAD_DIM:2 * DN_WIDTH + (h + 1) * DN_HEAD_DIM]
        beta = bg[:, h:h + 1]
        eg = jnp.exp(bg[:, DN_HEADS + h:DN_HEADS + h + 1])
        qk = jnp.sum(q * k, axis=-1, keepdims=True)
        outs = []
        for b in range(bt):
            s = s_ref[b, h]
            k_col = jnp.broadcast_to(k[b:b + 1], (DN_HEAD_DIM, DN_HEAD_DIM)).T
            q_col = jnp.broadcast_to(q[b:b + 1], (DN_HEAD_DIM, DN_HEAD_DIM)).T
            ks = jnp.sum(k_col * s, axis=0, keepdims=True)
            qs = jnp.sum(q_col * s, axis=0, keepdims=True)
            e = eg[b:b + 1]
            u = beta[b:b + 1] * (v[b:b + 1] - e * ks)
            outs.append(e * qs + qk[b:b + 1] * u)
            s_o[b, h] = e * s + k_col * u
        o = jnp.concatenate(outs, axis=0)
        y_o[:, cols] = _gated_norm(o, ng_ref[...], z_ref[:, cols])


def _gdn_step(qkv, sq, z, bg, s, layer, cw, ng, bt):
    n = qkv.shape[0]
    row = lambda w: pl.BlockSpec((bt, w), lambda i: (i, 0))
    const3 = lambda a, b_: pl.BlockSpec((None, a, b_), lambda i: (layer, 0, 0))
    sblk = pl.BlockSpec((bt, DN_HEADS, DN_HEAD_DIM, DN_HEAD_DIM), lambda i: (i, 0, 0, 0))
    return pl.pallas_call(
        _gdn_step_kernel,
        grid=(n // bt,),
        in_specs=[row(3 * DN_WIDTH), row(sq.shape[1]), row(DN_WIDTH), row(LANES), sblk,
                  const3(DN_CONV, 3 * DN_WIDTH), const3(1, DN_HEAD_DIM)],
        out_specs=[row(DN_WIDTH), row(sq.shape[1]), sblk],
        out_shape=[jax.ShapeDtypeStruct(z.shape, F32), jax.ShapeDtypeStruct(sq.shape, F32),
                   jax.ShapeDtypeStruct(s.shape, F32)],
        compiler_params=_cparams(("parallel",)),
        name="gated_delta_step",
    )(qkv, sq, z, bg, s, cw, ng)


def _mix_ffn_kernel(x_ref, ya_ref, yb_ref, yc_ref, gate1_ref, shift_ref, scale_ref, gate_ref, g_ref,
                    wo_ref, w1_ref, w2_ref, gf_ref, o_ref, *, final, hidden_chunk):
    a_end, b_end = POOL_WIDTH, POOL_WIDTH + DN_WIDTH
    mix = (_dot(ya_ref[...], wo_ref[0:a_end, :]) + _dot(yb_ref[...], wo_ref[a_end:b_end, :])
           + _dot(yc_ref[...], wo_ref[b_end:, :]))
    x = x_ref[...] + gate1_ref[...] * mix
    h = _mod_rmsnorm(x, g_ref[...], scale_ref[...], shift_ref[...]).astype(BF16)
    acc = jnp.zeros(x.shape, F32)
    for c in range(0, D_FF, hidden_chunk):
        a = jnp.maximum(jnp.dot(h, w1_ref[:, c:c + hidden_chunk], preferred_element_type=F32), 0.0)
        acc = acc + jnp.dot((a * a).astype(BF16), w2_ref[c:c + hidden_chunk, :],
                            preferred_element_type=F32)
    y = x + gate_ref[...] * acc
    if final:
        y = y * lax.rsqrt(jnp.mean(y * y, axis=-1, keepdims=True) + EPS) * gf_ref[...]
    o_ref[...] = y


def _mix_ffn(x, ya, yb, yc, mod, mode, layer, rows_per_batch, g2, w_out, w1, w2, g_final, final, tm):
    n = x.shape[0]
    row = lambda w: pl.BlockSpec((tm, w), lambda i: (i, 0))
    resident = lambda a, b: pl.BlockSpec((None, a, b), lambda i: (layer, 0, 0),
                                         pipeline_mode=pl.Buffered(1))
    return pl.pallas_call(
        functools.partial(_mix_ffn_kernel, final=final, hidden_chunk=1024),
        grid=(n // tm,),
        in_specs=[row(D_MODEL), row(POOL_WIDTH), row(DN_WIDTH), row(CONF_WIDTH),
                  _mod_spec(mode, 2, tm, rows_per_batch),
                  _mod_spec(mode, 3, tm, rows_per_batch),
                  _mod_spec(mode, 4, tm, rows_per_batch),
                  _mod_spec(mode, 5, tm, rows_per_batch),
                  pl.BlockSpec((None, 1, D_MODEL), lambda i: (layer, 0, 0)),
                  resident(D_MODEL, D_MODEL), resident(D_MODEL, D_FF), resident(D_FF, D_MODEL),
                  pl.BlockSpec((1, D_MODEL), lambda i: (0, 0))],
        out_specs=row(D_MODEL),
        out_shape=jax.ShapeDtypeStruct(x.shape, F32),
        compiler_params=_cparams(("parallel",)),
        name="mix_ffn",
    )(x, ya, yb, yc, mod, mod, mod, mod, g2, w_out, w1, w2, g_final)


def kernel(x_prompt, x_sample, state_pool, state_qkv_conv, state_delta, state_conv, c_prompt, c_sample,
           w_ada, b_ada, g_norm1, g_norm2, w_in, pool_w, pool_scale, qkv_conv_w, a_log, dt_bias,
           dn_norm_g, conf_dw_w, conf_dw_b, conf_ln_g, conf_ln_b, conf_pw_w, w_out, w_ff1, w_ff2, g_final):
    depth = w_in.shape[0]
    bp, seq, d = x_prompt.shape
    bs = x_sample.shape[0]
    assert d == D_MODEL and x_sample.shape[1] == 1

    w_in_r = w_in.astype(BF16)
    groups = pool_w.shape[1]
    pw_bd = jnp.einsum("lgcd,gh->lgchd", pool_w, jnp.eye(groups, dtype=pool_w.dtype))
    pw_bd = pw_bd.reshape(depth, POOL_WIDTH, POOL_WIDTH).astype(BF16)
    head_of = jnp.arange(CONF_WIDTH) // (CONF_WIDTH // CONF_HEADS)
    avg = jnp.where(head_of[:, None] == head_of[None, :], CONF_HEADS / CONF_WIDTH, 0.0).astype(BF16)
    tpos = jnp.arange(GDN_TILE)
    ltri = ((tpos[:, None] >= tpos[None, :])
            & (tpos[:, None] // GDN_CHUNK == tpos[None, :] // GDN_CHUNK)).astype(BF16)
    lane_id = jnp.arange(LANES)[:, None]
    head_id = jnp.arange(DN_WIDTH)[None, :] // DN_HEAD_DIM
    gdn_consts = (ltri, (lane_id == head_id).astype(BF16), (lane_id == DN_HEADS + head_id).astype(BF16))
    lane_pad = lambda a: jnp.pad(a, ((0, 0), (DN_HEADS, LANES - 2 * DN_HEADS)))[:, None, :]
    alog_row, dtb_row = lane_pad(a_log), lane_pad(dt_bias)
    r3 = lambda a: a[:, None, :]
    g1, g2, pscale, ng = r3(g_norm1), r3(g_norm2), r3(pool_scale), r3(dn_norm_g)
    cb, lng, lnb = r3(conf_dw_b), r3(conf_ln_g), r3(conf_ln_b)
    cpw = conf_pw_w.astype(BF16)
    w_out_b, w1_b, w2_b = w_out.astype(BF16), w_ff1.astype(BF16), w_ff2.astype(BF16)
    gf = g_final[None, :]

    mods = _ada(jnp.concatenate([c_prompt, c_sample], axis=0), w_ada, b_ada)
    mods_p = mods[:, :bp].reshape(depth, bp * 6, 1, d)
    mods_s = mods[:, bp:]

    xp = x_prompt.reshape(bp * seq, d)
    xs = x_sample.reshape(bs, d)
    tm_p, tt_mix, tt_gdn, bt_mix, bt_gdn = 512, 256, GDN_TILE, min(32, bs), min(8, bs)
    outs = [[] for _ in range(8)]
    for l in range(depth):
        last = l == depth - 1
        mp = mods_p[l]
        u, qkv, z, glu, bg = _inproj(xp, mp, "batch", l, seq, g1, w_in_r, alog_row, dtb_row, tm_p)
        ya, yc = _poolconf(u, glu, l, bp, seq, pw_bd, pscale, conf_dw_w, cb, lng, lnb, cpw, avg, tt_mix)
        yb, s_new = _gdn(qkv, z, bg, l, bp, seq, qkv_conv_w, ng, gdn_consts, tt_gdn)
        xp = _mix_ffn(xp, ya, yb, yc, mp, "batch", l, seq, g2, w_out_b, w1_b, w2_b, gf, last, tm_p)
        outs[0].append(u.reshape(bp, seq, -1)[:, seq - POOL_BUF:])
        outs[2].append(qkv.reshape(bp, seq, -1)[:, seq - (DN_CONV - 1):])
        outs[4].append(s_new)
        outs[6].append(glu.reshape(bp, seq, -1)[:, seq - (CONF_K - 1):])
        ms = mods_s[l]
        u, qkv, z, glu, bg = _inproj(xs, ms, "row", l, 1, g1, w_in_r, alog_row, dtb_row, bs)
        ya, yc, n_pool, n_conf = _poolconf_step(
            u, state_pool[l].reshape(bs, -1), glu, state_conv[l].reshape(bs, -1), l,
            pw_bd, pscale, conf_dw_w, cb, lng, lnb, cpw, avg, bt_mix)
        yb, n_qkv, s_new = _gdn_step(qkv, state_qkv_conv[l].reshape(bs, -1), z, bg, state_delta[l], l,
                                     qkv_conv_w, ng, bt_gdn)
        xs = _mix_ffn(xs, ya, yb, yc, ms, "row", l, 1, g2, w_out_b, w1_b, w2_b, gf, last, bs)
        outs[1].append(n_pool.reshape(bs, POOL_BUF, POOL_WIDTH))
        outs[3].append(n_qkv.reshape(bs, DN_CONV - 1, 3 * DN_WIDTH))
        outs[5].append(s_new)
        outs[7].append(n_conf.reshape(bs, CONF_K - 1, CONF_WIDTH))
    stacked = [jnp.stack(o) for o in outs]
    return (xp.reshape(bp, seq, d), xs.reshape(bs, 1, d)) + tuple(stacked)
```

```python
import functools

import jax
import jax.numpy as jnp
from jax import lax
from jax.experimental import pallas as pl
from jax.experimental.pallas import tpu as pltpu

F32 = jnp.float32
BF16 = jnp.bfloat16
EPS = 1e-6

D_MODEL = 1024
POOL_WIDTH = 256
POOL_WINDOWS = (2, 4, 8, 16)
POOL_BUF = 15
DN_WIDTH = 512
DN_HEAD_DIM = 128
DN_HEADS = 4
DN_CONV = 4
CONF_WIDTH = 256
CONF_HEADS = 4
CONF_K = 31
D_FF = 4 * D_MODEL
PAST_LEN = 16384
OFF_QKV = POOL_WIDTH
OFF_Z = OFF_QKV + 3 * DN_WIDTH
OFF_B = OFF_Z + DN_WIDTH
OFF_A = OFF_B + DN_HEADS
OFF_GLU = OFF_A + DN_HEADS
N_IN = OFF_GLU + 2 * CONF_WIDTH

LANES = 128
POOL_HIST = 32
CONF_HIST = 32
QKV_HIST = 8
GDN_CHUNK = 128
GDN_TILE = 512
SOLVE_BASE = 16
VMEM_LIMIT = 56 * 1024 * 1024


def _cparams(sem):
    return pltpu.CompilerParams(dimension_semantics=sem, vmem_limit_bytes=VMEM_LIMIT)


def _dot(a, b):
    return jnp.dot(a.astype(BF16), b.astype(BF16), preferred_element_type=F32)


def _lane_sum(x):
    ones = jnp.ones((x.shape[1], LANES), BF16)
    return jnp.dot(x.astype(BF16), ones, preferred_element_type=F32)


def _split_dot(x, m):
    hi = x.astype(BF16)
    lo = (x - hi.astype(F32)).astype(BF16)
    return (jnp.dot(hi, m, preferred_element_type=F32)
            + jnp.dot(lo, m, preferred_element_type=F32))


def _sigmoid(x):
    return 1.0 / (1.0 + jnp.exp(-x))


def _silu(x):
    return x * _sigmoid(x)


def _softplus(x):
    return jnp.maximum(x, 0.0) + jnp.log1p(jnp.exp(-jnp.abs(x)))


def _mod_rmsnorm(x, g, scale, shift):
    y = x * lax.rsqrt(jnp.mean(x * x, axis=-1, keepdims=True) + EPS)
    return (y * g) * (1.0 + scale) + shift


def _ada_kernel(c_ref, w_ref, b_ref, o_ref):
    o_ref[...] = _dot(_silu(c_ref[...]), w_ref[...]) + b_ref[...]


def _ada(c_all, w_ada, b_ada):
    depth, d, n = w_ada.shape
    rows = c_all.shape[0]
    tn = 1536
    return pl.pallas_call(
        _ada_kernel,
        grid=(depth, n // tn),
        in_specs=[pl.BlockSpec((rows, d), lambda l, j: (0, 0)),
                  pl.BlockSpec((None, d, tn), lambda l, j: (l, 0, j)),
                  pl.BlockSpec((None, 1, tn), lambda l, j: (l, 0, j))],
        out_specs=pl.BlockSpec((None, rows, tn), lambda l, j: (l, 0, j)),
        out_shape=jax.ShapeDtypeStruct((depth, rows, n), F32),
        compiler_params=_cparams(("arbitrary", "arbitrary")),
        name="ada_mod",
    )(c_all, w_ada, b_ada.reshape(depth, 1, n))


N_POOLCONF_REFS = 8


def _inproj_kernel(x_ref, shift_ref, scale_ref, g_ref, w_ref, alog_ref, dtb_ref, *rest, tm, tiles_per_seq):
    fused = tiles_per_seq is not None
    pc_refs, rest = (rest[:N_POOLCONF_REFS], rest[N_POOLCONF_REFS:]) if fused else ((), rest)
    pool_o, qkv_o, z_o, glu_o, bg_o = rest[:5]
    h = _mod_rmsnorm(x_ref[...], g_ref[...], scale_ref[...], shift_ref[...]).astype(BF16)

    def seg(lo, hi):
        return jnp.dot(h, w_ref[:, lo:hi], preferred_element_type=F32)

    pool_o[...] = seg(0, OFF_QKV)
    tail = seg(OFF_B, N_IN)
    glu_lo = OFF_GLU - OFF_B
    glu_o[...] = (tail[:, glu_lo:glu_lo + CONF_WIDTH]
                  * _sigmoid(tail[:, glu_lo + CONF_WIDTH:glu_lo + 2 * CONF_WIDTH]))
    ba = tail[:, :LANES]
    lane = lax.broadcasted_iota(jnp.int32, ba.shape, 1)
    beta = _sigmoid(ba)
    g = -jnp.exp(alog_ref[...]) * _softplus(ba + dtb_ref[...])
    bg_o[...] = jnp.where(lane < DN_HEADS, beta, g)
    if fused:
        ya_o, yc_o = rest[5:7]
        t = pl.program_id(0) % tiles_per_seq
        ya_o[...], yc_o[...] = _poolconf_tile(t, pool_o, glu_o, *pc_refs, *rest[7:], tt=tm, sub=64)
    qkv_o[...] = seg(OFF_QKV, OFF_Z)
    z_o[...] = seg(OFF_Z, OFF_B)


def _mod_spec(mode, chunk, tm, rows_per_batch):
    if mode == "batch":
        per = rows_per_batch // tm
        return pl.BlockSpec((None, 1, D_MODEL), lambda i: ((i // per) * 6 + chunk, 0, 0))
    return pl.BlockSpec((tm, D_MODEL), lambda i: (i, chunk))


def _inproj(x, mod, mode, layer, rows_per_batch, g1, w_in_r, alog_row, dtb_row, tm, poolconf_weights=None):
    n = x.shape[0]
    row = lambda w: pl.BlockSpec((tm, w), lambda i: (i, 0))
    const3 = lambda a, b: pl.BlockSpec((None, a, b), lambda i: (layer, 0, 0))
    outs = [POOL_WIDTH, 3 * DN_WIDTH, DN_WIDTH, CONF_WIDTH, LANES]
    in_specs = [row(D_MODEL),
                _mod_spec(mode, 0, tm, rows_per_batch),
                _mod_spec(mode, 1, tm, rows_per_batch),
                const3(1, D_MODEL),
                pl.BlockSpec((None, D_MODEL, N_IN), lambda i: (layer, 0, 0), pipeline_mode=pl.Buffered(1)),
                const3(1, LANES), const3(1, LANES)]
    args = (x, mod, mod, g1, w_in_r, alog_row, dtb_row)
    if poolconf_weights is None:
        tiles_per_seq, scratch, sem = None, [], "parallel"
    else:
        tiles_per_seq, scratch, sem = rows_per_batch // tm, _poolconf_scratch(tm), "arbitrary"
        in_specs += [const3(POOL_WIDTH, POOL_WIDTH), const3(1, POOL_WIDTH),
                     const3(CONF_K, CONF_WIDTH), const3(1, CONF_WIDTH),
                     const3(1, CONF_WIDTH), const3(1, CONF_WIDTH), const3(CONF_WIDTH, CONF_WIDTH),
                     pl.BlockSpec((CONF_WIDTH, CONF_WIDTH), lambda i: (0, 0))]
        args += tuple(poolconf_weights)
        outs += [POOL_WIDTH, CONF_WIDTH]
    return pl.pallas_call(
        functools.partial(_inproj_kernel, tm=tm, tiles_per_seq=tiles_per_seq),
        grid=(n // tm,),
        in_specs=in_specs,
        out_specs=[row(w) for w in outs],
        out_shape=[jax.ShapeDtypeStruct((n, w), F32) for w in outs],
        scratch_shapes=scratch,
        compiler_params=_cparams((sem,)),
        name="in_proj",
    )(*args)


def _pool_delta(win, u, pos, half):
    w_lo, w_hi = POOL_WINDOWS[2 * half], POOL_WINDOWS[2 * half + 1]
    a = win(0)
    for j in range(1, w_lo):
        a = a + win(j)
    b = win(w_lo)
    for j in range(w_lo + 1, w_hi):
        b = b + win(j)
    lane = lax.broadcasted_iota(jnp.int32, u.shape, 1)
    upper = lane >= LANES // 2
    s = a + jnp.where(upper, b, 0.0)
    cnt = jnp.where(upper, jnp.minimum(pos + 1, w_hi), jnp.minimum(pos + 1, w_lo)).astype(F32)
    return s / cnt - u


def _conf_post(dc, avg, lng, lnb, cpw):
    mu = _split_dot(dc, avg)
    xc = dc - mu
    var = _split_dot(xc * xc, avg)
    dn = xc * lax.rsqrt(var + EPS) * lng + lnb
    return _dot(_silu(dn), cpw)


def _poolconf_tile(t, u_ref, glu_ref, pw_ref, ps_ref, cw_ref, cb_ref, lng_ref, lnb_ref, cpw_ref,
                   avg_ref, epool, econf, pa, pb, erot, *, tt, sub):
    n = POOL_HIST + tt

    @pl.when(t == 0)
    def _():
        epool[0:POOL_HIST, :] = jnp.zeros((POOL_HIST, POOL_WIDTH), F32)
        econf[0:CONF_HIST, :] = jnp.zeros((CONF_HIST, CONF_WIDTH), F32)

    @pl.when(t > 0)
    def _():
        epool[0:POOL_HIST, :] = epool[tt:tt + POOL_HIST, :]
        econf[0:CONF_HIST, :] = econf[tt:tt + CONF_HIST, :]

    epool[POOL_HIST:, :] = u_ref[...]
    econf[CONF_HIST:, :] = glu_ref[...]

    pos = t * tt + lax.broadcasted_iota(jnp.int32, (tt, 1), 0)
    upper = lax.broadcasted_iota(jnp.int32, (tt, LANES), 1) >= LANES // 2
    halves = []
    for half in range(2):
        cols = slice(half * LANES, (half + 1) * LANES)
        w_lo, w_hi = POOL_WINDOWS[2 * half], POOL_WINDOWS[2 * half + 1]
        pa[8:n, :] = epool[8:n, cols] + epool[7:n - 1, cols]
        pb[16:n, :] = pa[16:n, :] + pa[14:n - 2, :]
        if half == 0:
            s_lo, s_hi = pa[POOL_HIST:n, :], pb[POOL_HIST:n, :]
        else:
            pa[24:n, :] = pb[24:n, :] + pb[20:n - 4, :]
            s_lo = pa[POOL_HIST:n, :]
            s_hi = s_lo + pa[POOL_HIST - 8:n - 8, :]
        cnt = jnp.where(upper, jnp.minimum(pos + 1, w_hi), jnp.minimum(pos + 1, w_lo)).astype(F32)
        halves.append(jnp.where(upper, s_hi, s_lo) / cnt - epool[POOL_HIST:n, cols])
    d = jnp.concatenate(halves, axis=1)
    ya = _dot(d, pw_ref[...]) * ps_ref[...]

    m = CONF_HIST + tt - 8
    for r in range(1, 8):
        erot[r - 1, 0:m, :] = econf[r:r + m, :]
    first = CONF_HIST - (CONF_K - 1)
    dc_blocks = []
    for r0 in range(0, tt, sub):
        acc = cb_ref[...]
        for j in range(CONF_K):
            a8, r = divmod(first + j, 8)
            lo = 8 * a8 + r0
            rows = econf[lo:lo + sub, :] if r == 0 else erot[r - 1, lo:lo + sub, :]
            acc = acc + cw_ref[j:j + 1, :] * rows
        dc_blocks.append(acc)
    dc = jnp.concatenate(dc_blocks, axis=0)
    return ya, _conf_post(dc, avg_ref[...], lng_ref[...], lnb_ref[...], cpw_ref[...])


def _poolconf_scratch(tt):
    return [pltpu.VMEM((POOL_HIST + tt, POOL_WIDTH), F32),
            pltpu.VMEM((CONF_HIST + tt, CONF_WIDTH), F32),
            pltpu.VMEM((POOL_HIST + tt, LANES), F32),
            pltpu.VMEM((POOL_HIST + tt, LANES), F32),
            pltpu.VMEM((7, CONF_HIST + tt, CONF_WIDTH), F32)]


def _poolconf_step_kernel(u_ref, sp_ref, glu_ref, sc_ref, pw_ref, ps_ref, cw_ref, cb_ref, lng_ref,
                          lnb_ref, cpw_ref, avg_ref, *rest, pos0):
    ya_o, yc_o, np_o, nc_o = rest[-4:]
    u = u_ref[...]
    rows = u.shape[0]
    pos = jnp.full((rows, 1), pos0, jnp.int32)
    halves = []
    for half in range(2):
        def win(j, half=half):
            if j == 0:
                return u[:, half * LANES:(half + 1) * LANES]
            lo = (POOL_BUF - j) * POOL_WIDTH + half * LANES
            return sp_ref[:, lo:lo + LANES]
        halves.append(_pool_delta(win, win(0), pos, half))
    d = jnp.concatenate(halves, axis=1)
    ya_o[...] = _dot(d, pw_ref[...]) * ps_ref[...]
    np_o[:, 0:(POOL_BUF - 1) * POOL_WIDTH] = sp_ref[:, POOL_WIDTH:]
    np_o[:, (POOL_BUF - 1) * POOL_WIDTH:] = u

    glu = glu_ref[...]
    acc = cb_ref[...] + cw_ref[CONF_K - 1:CONF_K, :] * glu
    for j in range(CONF_K - 1):
        acc = acc + cw_ref[j:j + 1, :] * sc_ref[:, j * CONF_WIDTH:(j + 1) * CONF_WIDTH]
    yc_o[...] = _conf_post(acc, avg_ref[...], lng_ref[...], lnb_ref[...], cpw_ref[...])
    nc_o[:, 0:(CONF_K - 2) * CONF_WIDTH] = sc_ref[:, CONF_WIDTH:]
    nc_o[:, (CONF_K - 2) * CONF_WIDTH:] = glu


def _poolconf_step(u, sp_all, glu, sc_all, prev, layer, pw_bd, pscale, cw, cb, lng, lnb, cpw, avg, bt):
    n = u.shape[0]
    row = lambda w: pl.BlockSpec((bt, w), lambda i: (i, 0))
    const3 = lambda a, b_: pl.BlockSpec((None, a, b_), lambda i: (layer, 0, 0))
    state = lambda a: pl.BlockSpec((None, bt, a.shape[2]), lambda i: (layer, i, 0))
    carry_specs, aliases = _carried(prev, 12, 2)
    return pl.pallas_call(
        functools.partial(_poolconf_step_kernel, pos0=PAST_LEN),
        grid=(n // bt,),
        in_specs=[row(POOL_WIDTH), state(sp_all), row(CONF_WIDTH), state(sc_all),
                  const3(POOL_WIDTH, POOL_WIDTH), const3(1, POOL_WIDTH),
                  const3(CONF_K, CONF_WIDTH), const3(1, CONF_WIDTH),
                  const3(1, CONF_WIDTH), const3(1, CONF_WIDTH),
                  const3(CONF_WIDTH, CONF_WIDTH),
                  pl.BlockSpec((CONF_WIDTH, CONF_WIDTH), lambda i: (0, 0))] + carry_specs,
        out_specs=[row(POOL_WIDTH), row(CONF_WIDTH), state(sp_all), state(sc_all)],
        out_shape=[jax.ShapeDtypeStruct((n, POOL_WIDTH), F32), jax.ShapeDtypeStruct((n, CONF_WIDTH), F32),
                   jax.ShapeDtypeStruct(sp_all.shape, F32), jax.ShapeDtypeStruct(sc_all.shape, F32)],
        input_output_aliases=aliases,
        compiler_params=_cparams(("parallel",)),
        name="pool_conf_step",
    )(u, sp_all, glu, sc_all, pw_bd, pscale, cw, cb, lng, lnb, cpw, avg, *(prev or ()))


def _l2norm(x):
    return x * lax.rsqrt(jnp.sum(x * x, axis=-1, keepdims=True) + EPS)


def _gated_norm(o, ng, z):
    return o * lax.rsqrt(jnp.mean(o * o, axis=-1, keepdims=True) + EPS) * ng * _silu(z)


def _tri_masks(c):
    ri = lax.broadcasted_iota(jnp.int32, (c, c), 0)
    ci = lax.broadcasted_iota(jnp.int32, (c, c), 1)
    masks = [ri // SOLVE_BASE == ci // SOLVE_BASE]
    size = SOLVE_BASE
    while size < c:
        masks.append((ri // (2 * size) == ci // (2 * size)) & (ri // size != ci // size))
        size *= 2
    return masks


def _tri_solve(bs, rhss, c, masks):
    ns = [jnp.where(masks[0], b, 0.0) for b in bs]
    ps = [_dot(n, n) for n in ns]
    power = 2
    while power < SOLVE_BASE:
        if 2 * power < SOLVE_BASE:
            sts = [_dot(jnp.concatenate([n, p], axis=0), p) for n, p in zip(ns, ps)]
            ns = [n + p + st[:c] for n, p, st in zip(ns, ps, sts)]
            ps = [st[c:] for st in sts]
        else:
            ns = [n + p + _dot(n, p) for n, p in zip(ns, ps)]
        power *= 2
    for m in masks[1:]:
        ls = [jnp.where(m, b, 0.0) for b in bs]
        tls = [l + _dot(n, l) for n, l in zip(ns, ls)]
        ns = [n + tl + _dot(tl, n) for n, tl in zip(ns, tls)]
    return [rhs + _dot(n, rhs) for n, rhs in zip(ns, rhss)]


def _gdn_kernel(qkv_ref, z_ref, bg_ref, cw_ref, ng_ref, ltri_ref, selb_ref, selg_ref, y_o, s_o,
                eq, eqs, s_scr, *, tt, chunk):
    t = pl.program_id(1)
    width = 3 * DN_WIDTH
    dk = DN_HEAD_DIM

    @pl.when(t == 0)
    def _():
        eq[0:QKV_HIST, :] = jnp.zeros((QKV_HIST, width), F32)
        s_scr[...] = jnp.zeros(s_scr.shape, F32)

    @pl.when(t > 0)
    def _():
        eq[0:QKV_HIST, :] = eq[tt:tt + QKV_HIST, :]

    eq[QKV_HIST:, :] = qkv_ref[...]

    for r in range(1, DN_CONV):
        eqs[r - 1] = eq[QKV_HIST - r:QKV_HIST - r + tt, :]

    def conv_cols(lo):
        acc = cw_ref[DN_CONV - 1:DN_CONV, lo:lo + LANES] * eq[QKV_HIST:QKV_HIST + tt, lo:lo + LANES]
        for r in range(1, DN_CONV):
            acc = acc + cw_ref[DN_CONV - 1 - r:DN_CONV - r, lo:lo + LANES] * eqs[r - 1, :, lo:lo + LANES]
        return _silu(acc)

    def split3(x):
        hi = x.astype(BF16)
        r = x - hi.astype(F32)
        mid = r.astype(BF16)
        return hi, mid, (r - mid.astype(F32)).astype(BF16)

    def dot3(a, pieces, left):
        return sum(jnp.dot(a, p, preferred_element_type=F32) if left else
                   jnp.dot(p, a, preferred_element_type=F32) for p in pieces)

    bg3 = split3(bg_ref[...])
    gc = dot3(ltri_ref[...], bg3, True)
    gt = gc.T
    beta_d = dot3(selb_ref[...], bg3, False)
    g_d = dot3(selg_ref[...], split3(gc), False)
    eg_d = jnp.exp(g_d)
    rc = lax.broadcasted_iota(jnp.int32, (chunk, chunk), 0)
    cc = lax.broadcasted_iota(jnp.int32, (chunk, chunk), 1)
    incl = rc >= cc
    strict = rc > cc
    masks = _tri_masks(chunk)

    heads = range(DN_HEADS)
    chunks = range(tt // chunk)
    rows = [slice(c * chunk, (c + 1) * chunk) for c in chunks]
    def l2n(x):
        return x * lax.rsqrt(_lane_sum(x * x) + EPS)

    q = [l2n(conv_cols(h * dk)) * (dk ** -0.5) for h in heads]
    k = [l2n(conv_cols(DN_WIDTH + h * dk)) for h in heads]
    v = [conv_cols(2 * DN_WIDTH + h * dk) for h in heads]
    cols = [slice(h * dk, (h + 1) * dk) for h in heads]
    beta = [beta_d[:, cols[h]] for h in heads]
    gcol = [g_d[:, cols[h]] for h in heads]
    eg = [eg_d[:, cols[h]] for h in heads]
    grow = [gt[DN_HEADS + h:DN_HEADS + h + 1, :] for h in heads]
    qg = [q[h] * eg[h] for h in heads]
    rhs_all = [jnp.concatenate([v[h] * beta[h], k[h] * (beta[h] * eg[h])], axis=1) for h in heads]

    pairs = [(h, c) for h in heads for c in chunks]
    kq = [_dot(jnp.concatenate([k[h][rows[c]], q[h][rows[c]]], axis=0), k[h][rows[c]].T)
          for h, c in pairs]
    decay = [jnp.where(incl, jnp.exp(jnp.where(incl, gcol[h][rows[c]] - grow[h][:, rows[c]], 0.0)), 0.0)
             for h, c in pairs]
    bmat = [jnp.where(strict, -(beta[h][rows[c]] * kq[i][:chunk] * decay[i]), 0.0)
            for i, (h, c) in enumerate(pairs)]
    qk = [kq[i][chunk:] * decay[i] for i in range(len(pairs))]
    sol = _tri_solve(bmat, [rhs_all[h][rows[c]] for h, c in pairs], chunk, masks)
    g_last = [gcol[h][(c + 1) * chunk - 1:(c + 1) * chunk] for h, c in pairs]
    k_tail = [k[h][rows[c]] * jnp.exp(g_last[i] - gcol[h][rows[c]]) for i, (h, c) in enumerate(pairs)]

    s = [s_scr[h] for h in heads]
    o = [[] for _ in heads]
    for c in chunks:
        idx = [pairs.index((h, c)) for h in heads]
        ws = [_dot(jnp.concatenate([sol[idx[h]][:, dk:], qg[h][rows[c]]], axis=0), s[h]) for h in heads]
        u = [sol[idx[h]][:, :dk] - ws[h][:chunk] for h in heads]
        for h in heads:
            o[h].append(ws[h][chunk:] + _dot(qk[idx[h]], u[h]))
        s = [s[h] * jnp.exp(g_last[idx[h]]) + _dot(k_tail[idx[h]].T, u[h]) for h in heads]
    for h in heads:
        s_scr[h] = s[h]
    y = []
    for h in heads:
        oh = jnp.concatenate(o[h], axis=0)
        rms = lax.rsqrt(_lane_sum(oh * oh) * (1.0 / dk) + EPS)
        y.append(oh * rms * ng_ref[...] * _silu(z_ref[:, h * dk:(h + 1) * dk]))
    y_o[...] = jnp.concatenate(y, axis=1)

    @pl.when(t == pl.num_programs(1) - 1)
    def _():
        s_o[...] = s_scr[...]


def _gdn(qkv, z, bg, layer, batch, seq, cw, ng, consts, tt):
    assert GDN_CHUNK == LANES and DN_HEAD_DIM == LANES
    per = seq // tt
    row = lambda w: pl.BlockSpec((tt, w), lambda b, t: (b * per + t, 0))
    const3 = lambda a, b_: pl.BlockSpec((None, a, b_), lambda b, t: (layer, 0, 0))
    state = (DN_HEADS, DN_HEAD_DIM, DN_HEAD_DIM)
    return pl.pallas_call(
        functools.partial(_gdn_kernel, tt=tt, chunk=GDN_CHUNK),
        grid=(batch, per),
        in_specs=[row(3 * DN_WIDTH), row(DN_WIDTH), row(LANES),
                  const3(DN_CONV, 3 * DN_WIDTH), const3(1, DN_HEAD_DIM),
                  pl.BlockSpec((tt, tt), lambda b, t: (0, 0)),
                  pl.BlockSpec((LANES, DN_WIDTH), lambda b, t: (0, 0)),
                  pl.BlockSpec((LANES, DN_WIDTH), lambda b, t: (0, 0))],
        out_specs=[row(DN_WIDTH),
                   pl.BlockSpec((None,) + state, lambda b, t: (b, 0, 0, 0))],
        out_shape=[jax.ShapeDtypeStruct(z.shape, F32),
                   jax.ShapeDtypeStruct((batch,) + state, F32)],
        scratch_shapes=[pltpu.VMEM((QKV_HIST + tt, 3 * DN_WIDTH), F32),
                        pltpu.VMEM((DN_CONV - 1, tt, 3 * DN_WIDTH), F32),
                        pltpu.VMEM(state, F32)],
        compiler_params=_cparams(("parallel", "arbitrary")),
        name="gated_delta",
    )(qkv, z, bg, cw, ng, *consts)


def _gdn_step_kernel(qkv_ref, sq_ref, z_ref, bg_ref, s_ref, cw_ref, ng_ref, *rest):
    y_o, nq_o, s_o = rest[-3:]
    width = 3 * DN_WIDTH
    qkv = qkv_ref[...]
    bt = qkv.shape[0]
    acc = cw_ref[DN_CONV - 1:DN_CONV, :] * qkv
    for j in range(DN_CONV - 1):
        acc = acc + cw_ref[j:j + 1, :] * sq_ref[:, j * width:(j + 1) * width]
    act = _silu(acc)
    nq_o[:, 0:(DN_CONV - 2) * width] = sq_ref[:, width:]
    nq_o[:, (DN_CONV - 2) * width:] = qkv
    bg = bg_ref[...]
    for h in range(DN_HEADS):
        cols = slice(h * DN_HEAD_DIM, (h + 1) * DN_HEAD_DIM)
        q = _l2norm(act[:, cols]) * (DN_HEAD_DIM ** -0.5)
        k = _l2norm(act[:, DN_WIDTH + h * DN_HEAD_DIM:DN_WIDTH + (h + 1) * DN_HEAD_DIM])
        v = act[:, 2 * DN_WIDTH + h * DN_HEAD_DIM:2 * DN_WIDTH + (h + 1) * DN_HEAD_DIM]
        beta = bg[:, h:h + 1]
        eg = jnp.exp(bg[:, DN_HEADS + h:DN_HEADS + h + 1])
        qk = jnp.sum(q * k, axis=-1, keepdims=True)
        outs = []
        for b in range(bt):
            s = s_ref[b, h]
            k_col = jnp.broadcast_to(k[b:b + 1], (DN_HEAD_DIM, DN_HEAD_DIM)).T
            q_col = jnp.broadcast_to(q[b:b + 1], (DN_HEAD_DIM, DN_HEAD_DIM)).T
            ks = jnp.sum(k_col * s, axis=0, keepdims=True)
            qs = jnp.sum(q_col * s, axis=0, keepdims=True)
            e = eg[b:b + 1]
            u = beta[b:b + 1] * (v[b:b + 1] - e * ks)
            outs.append(e * qs + qk[b:b + 1] * u)
            s_o[b, h] = e * s + k_col * u
        o = jnp.concatenate(outs, axis=0)
        y_o[:, cols] = _gated_norm(o, ng_ref[...], z_ref[:, cols])


def _carried(prev, first_input, first_output):
    if prev is None:
        return [], {}
    specs = [pl.BlockSpec(memory_space=pl.ANY) for _ in prev]
    return specs, {first_input + j: first_output + j for j in range(len(prev))}


def _gdn_step(qkv, sq_all, z, bg, s_all, prev, layer, cw, ng, bt):
    n = qkv.shape[0]
    row = lambda w: pl.BlockSpec((bt, w), lambda i: (i, 0))
    const3 = lambda a, b_: pl.BlockSpec((None, a, b_), lambda i: (layer, 0, 0))
    sblk = pl.BlockSpec((None, bt, DN_HEADS, DN_HEAD_DIM, DN_HEAD_DIM), lambda i: (layer, i, 0, 0, 0))
    qblk = pl.BlockSpec((None, bt, sq_all.shape[2]), lambda i: (layer, i, 0))
    carry_specs, aliases = _carried(prev, 7, 1)
    return pl.pallas_call(
        _gdn_step_kernel,
        grid=(n // bt,),
        in_specs=[row(3 * DN_WIDTH), qblk, row(DN_WIDTH), row(LANES), sblk,
                  const3(DN_CONV, 3 * DN_WIDTH), const3(1, DN_HEAD_DIM)] + carry_specs,
        out_specs=[row(DN_WIDTH), qblk, sblk],
        out_shape=[jax.ShapeDtypeStruct(z.shape, F32), jax.ShapeDtypeStruct(sq_all.shape, F32),
                   jax.ShapeDtypeStruct(s_all.shape, F32)],
        input_output_aliases=aliases,
        compiler_params=_cparams(("parallel",)),
        name="gated_delta_step",
    )(qkv, sq_all, z, bg, s_all, cw, ng, *(prev or ()))


def _mix_ffn_kernel(x_ref, ya_ref, yb_ref, yc_ref, gate1_ref, shift_ref, scale_ref, gate_ref, g_ref,
                    wo_ref, w1_ref, w2_ref, gf_ref, o_ref, *, final, hidden_chunk):
    a_end, b_end = POOL_WIDTH, POOL_WIDTH + DN_WIDTH
    mix = (_dot(ya_ref[...], wo_ref[0:a_end, :]) + _dot(yb_ref[...], wo_ref[a_end:b_end, :])
           + _dot(yc_ref[...], wo_ref[b_end:, :]))
    x = x_ref[...] + gate1_ref[...] * mix
    h = _mod_rmsnorm(x, g_ref[...], scale_ref[...], shift_ref[...]).astype(BF16)
    acc = jnp.zeros(x.shape, F32)
    for c in range(0, D_FF, hidden_chunk):
        a = jnp.maximum(jnp.dot(h, w1_ref[:, c:c + hidden_chunk], preferred_element_type=F32), 0.0)
        acc = acc + jnp.dot((a * a).astype(BF16), w2_ref[c:c + hidden_chunk, :],
                            preferred_element_type=F32)
    y = x + gate_ref[...] * acc
    if final:
        y = y * lax.rsqrt(jnp.mean(y * y, axis=-1, keepdims=True) + EPS) * gf_ref[...]
    o_ref[...] = y


def _mix_ffn(x, ya, yb, yc, mod, mode, layer, rows_per_batch, g2, w_out, w1, w2, g_final, final, tm):
    n = x.shape[0]
    row = lambda w: pl.BlockSpec((tm, w), lambda i: (i, 0))
    resident = lambda a, b: pl.BlockSpec((None, a, b), lambda i: (layer, 0, 0),
                                         pipeline_mode=pl.Buffered(1))
    return pl.pallas_call(
        functools.partial(_mix_ffn_kernel, final=final, hidden_chunk=1024),
        grid=(n // tm,),
        in_specs=[row(D_MODEL), row(POOL_WIDTH), row(DN_WIDTH), row(CONF_WIDTH),
                  _mod_spec(mode, 2, tm, rows_per_batch),
                  _mod_spec(mode, 3, tm, rows_per_batch),
                  _mod_spec(mode, 4, tm, rows_per_batch),
                  _mod_spec(mode, 5, tm, rows_per_batch),
                  pl.BlockSpec((None, 1, D_MODEL), lambda i: (layer, 0, 0)),
                  resident(D_MODEL, D_MODEL), resident(D_MODEL, D_FF), resident(D_FF, D_MODEL),
                  pl.BlockSpec((1, D_MODEL), lambda i: (0, 0))],
        out_specs=row(D_MODEL),
        out_shape=jax.ShapeDtypeStruct(x.shape, F32),
        compiler_params=_cparams(("parallel",)),
        name="mix_ffn",
    )(x, ya, yb, yc, mod, mod, mod, mod, g2, w_out, w1, w2, g_final)


def kernel(x_prompt, x_sample, state_pool, state_qkv_conv, state_delta, state_conv, c_prompt, c_sample,
           w_ada, b_ada, g_norm1, g_norm2, w_in, pool_w, pool_scale, qkv_conv_w, a_log, dt_bias,
           dn_norm_g, conf_dw_w, conf_dw_b, conf_ln_g, conf_ln_b, conf_pw_w, w_out, w_ff1, w_ff2, g_final):
    depth = w_in.shape[0]
    bp, seq, d = x_prompt.shape
    bs = x_sample.shape[0]
    assert d == D_MODEL and x_sample.shape[1] == 1

    w_in_r = w_in.astype(BF16)
    groups = pool_w.shape[1]
    pw_bd = jnp.einsum("lgcd,gh->lgchd", pool_w, jnp.eye(groups, dtype=pool_w.dtype))
    pw_bd = pw_bd.reshape(depth, POOL_WIDTH, POOL_WIDTH).astype(BF16)
    head_of = jnp.arange(CONF_WIDTH) // (CONF_WIDTH // CONF_HEADS)
    avg = jnp.where(head_of[:, None] == head_of[None, :], CONF_HEADS / CONF_WIDTH, 0.0).astype(BF16)
    tpos = jnp.arange(GDN_TILE)
    ltri = ((tpos[:, None] >= tpos[None, :])
            & (tpos[:, None] // GDN_CHUNK == tpos[None, :] // GDN_CHUNK)).astype(BF16)
    lane_id = jnp.arange(LANES)[:, None]
    head_id = jnp.arange(DN_WIDTH)[None, :] // DN_HEAD_DIM
    gdn_consts = (ltri, (lane_id == head_id).astype(BF16), (lane_id == DN_HEADS + head_id).astype(BF16))
    lane_pad = lambda a: jnp.pad(a, ((0, 0), (DN_HEADS, LANES - 2 * DN_HEADS)))[:, None, :]
    alog_row, dtb_row = lane_pad(a_log), lane_pad(dt_bias)
    r3 = lambda a: a[:, None, :]
    g1, g2, pscale, ng = r3(g_norm1), r3(g_norm2), r3(pool_scale), r3(dn_norm_g)
    cb, lng, lnb = r3(conf_dw_b), r3(conf_ln_g), r3(conf_ln_b)
    cpw = conf_pw_w.astype(BF16)
    w_out_b, w1_b, w2_b = w_out.astype(BF16), w_ff1.astype(BF16), w_ff2.astype(BF16)
    gf = g_final[None, :]

    mods = _ada(jnp.concatenate([c_prompt, c_sample], axis=0), w_ada, b_ada)
    mods_p = mods[:, :bp].reshape(depth, bp * 6, 1, d)
    mods_s = mods[:, bp:]

    xp = x_prompt.reshape(bp * seq, d)
    xs = x_sample.reshape(bs, d)
    tm_p, tt_gdn, bt_mix, bt_gdn = 512, GDN_TILE, min(32, bs), min(8, bs)
    pc_weights = (pw_bd, pscale, conf_dw_w, cb, lng, lnb, cpw, avg)
    outs = {0: [], 2: [], 4: [], 6: []}
    sp_all = state_pool.reshape(depth, bs, -1)
    sc_all = state_conv.reshape(depth, bs, -1)
    sq_all = state_qkv_conv.reshape(depth, bs, -1)
    pc_carry = gdn_carry = None
    for l in range(depth):
        last = l == depth - 1
        mp = mods_p[l]
        u, qkv, z, glu, bg, ya, yc = _inproj(xp, mp, "batch", l, seq, g1, w_in_r, alog_row, dtb_row, tm_p,
                                             poolconf_weights=pc_weights)
        yb, s_new = _gdn(qkv, z, bg, l, bp, seq, qkv_conv_w, ng, gdn_consts, tt_gdn)
        xp = _mix_ffn(xp, ya, yb, yc, mp, "batch", l, seq, g2, w_out_b, w1_b, w2_b, gf, last, tm_p)
        outs[0].append(u.reshape(bp, seq, -1)[:, seq - POOL_BUF:])
        outs[2].append(qkv.reshape(bp, seq, -1)[:, seq - (DN_CONV - 1):])
        outs[4].append(s_new)
        outs[6].append(glu.reshape(bp, seq, -1)[:, seq - (CONF_K - 1):])
        ms = mods_s[l]
        u, qkv, z, glu, bg = _inproj(xs, ms, "row", l, 1, g1, w_in_r, alog_row, dtb_row, bs)
        ya, yc, *pc_carry = _poolconf_step(u, sp_all, glu, sc_all, pc_carry, l, *pc_weights, bt_mix)
        yb, *gdn_carry = _gdn_step(qkv, sq_all, z, bg, state_delta, gdn_carry, l, qkv_conv_w, ng, bt_gdn)
        xs = _mix_ffn(xs, ya, yb, yc, ms, "row", l, 1, g2, w_out_b, w1_b, w2_b, gf, last, bs)
    n_pool, n_conf = pc_carry
    n_qkv, n_delta = gdn_carry
    return (xp.reshape(bp, seq, d), xs.reshape(bs, 1, d),
            jnp.stack(outs[0]), n_pool.reshape(state_pool.shape),
            jnp.stack(outs[2]), n_qkv.reshape(state_qkv_conv.shape),
            jnp.stack(outs[4]), n_delta,
            jnp.stack(outs[6]), n_conf.reshape(state_conv.shape))
```

```python
import functools

import jax
import jax.numpy as jnp
from jax import lax
from jax.experimental import pallas as pl
from jax.experimental.pallas import tpu as pltpu

F32 = jnp.float32
BF16 = jnp.bfloat16
EPS = 1e-6

D_MODEL = 1024
POOL_WIDTH = 256
POOL_WINDOWS = (2, 4, 8, 16)
POOL_BUF = 15
DN_WIDTH = 512
DN_HEAD_DIM = 128
DN_HEADS = 4
DN_CONV = 4
CONF_WIDTH = 256
CONF_HEADS = 4
CONF_K = 31
D_FF = 4 * D_MODEL
PAST_LEN = 16384
OFF_QKV = POOL_WIDTH
OFF_Z = OFF_QKV + 3 * DN_WIDTH
OFF_B = OFF_Z + DN_WIDTH
OFF_A = OFF_B + DN_HEADS
OFF_GLU = OFF_A + DN_HEADS
N_IN = OFF_GLU + 2 * CONF_WIDTH

LANES = 128
POOL_HIST = 32
CONF_HIST = 32
QKV_HIST = 8
GDN_CHUNK = 128
GDN_TILE = 256
GDN_SEQS = 2
SOLVE_BASE = 16
VMEM_LIMIT = 56 * 1024 * 1024


def _cparams(sem):
    return pltpu.CompilerParams(dimension_semantics=sem, vmem_limit_bytes=VMEM_LIMIT)


def _dot(a, b):
    return jnp.dot(a.astype(BF16), b.astype(BF16), preferred_element_type=F32)


def _lane_sum(x):
    ones = jnp.ones((x.shape[1], LANES), BF16)
    return jnp.dot(x.astype(BF16), ones, preferred_element_type=F32)


def _split_dot(x, m):
    hi = x.astype(BF16)
    lo = (x - hi.astype(F32)).astype(BF16)
    return (jnp.dot(hi, m, preferred_element_type=F32)
            + jnp.dot(lo, m, preferred_element_type=F32))


def _sigmoid(x):
    return 1.0 / (1.0 + jnp.exp(-x))


def _silu(x):
    return x * _sigmoid(x)


def _softplus(x):
    return jnp.maximum(x, 0.0) + jnp.log1p(jnp.exp(-jnp.abs(x)))


def _mod_rmsnorm(x, g, scale, shift):
    y = x * lax.rsqrt(jnp.mean(x * x, axis=-1, keepdims=True) + EPS)
    return (y * g) * (1.0 + scale) + shift


def _ada_kernel(c_ref, w_ref, b_ref, o_ref):
    o_ref[...] = _dot(_silu(c_ref[...]), w_ref[...]) + b_ref[...]


def _ada(c_all, w_ada, b_ada):
    depth, d, n = w_ada.shape
    rows = c_all.shape[0]
    tn = 1536
    return pl.pallas_call(
        _ada_kernel,
        grid=(depth, n // tn),
        in_specs=[pl.BlockSpec((rows, d), lambda l, j: (0, 0)),
                  pl.BlockSpec((None, d, tn), lambda l, j: (l, 0, j)),
                  pl.BlockSpec((None, 1, tn), lambda l, j: (l, 0, j))],
        out_specs=pl.BlockSpec((None, rows, tn), lambda l, j: (l, 0, j)),
        out_shape=jax.ShapeDtypeStruct((depth, rows, n), F32),
        compiler_params=_cparams(("arbitrary", "arbitrary")),
        name="ada_mod",
    )(c_all, w_ada, b_ada.reshape(depth, 1, n))


N_POOLCONF_REFS = 8


def _inproj_kernel(x_ref, shift_ref, scale_ref, g_ref, w_ref, alog_ref, dtb_ref, *rest, tm, tiles_per_seq):
    fused = tiles_per_seq is not None
    pc_refs, rest = (rest[:N_POOLCONF_REFS], rest[N_POOLCONF_REFS:]) if fused else ((), rest)
    pool_o, qkv_o, z_o, glu_o, bg_o = rest[:5]
    h = _mod_rmsnorm(x_ref[...], g_ref[...], scale_ref[...], shift_ref[...]).astype(BF16)

    def seg(lo, hi):
        return jnp.dot(h, w_ref[:, lo:hi], preferred_element_type=F32)

    pool_o[...] = seg(0, OFF_QKV)
    tail = seg(OFF_B, N_IN)
    glu_lo = OFF_GLU - OFF_B
    glu_o[...] = (tail[:, glu_lo:glu_lo + CONF_WIDTH]
                  * _sigmoid(tail[:, glu_lo + CONF_WIDTH:glu_lo + 2 * CONF_WIDTH]))
    ba = tail[:, :LANES]
    lane = lax.broadcasted_iota(jnp.int32, ba.shape, 1)
    beta = _sigmoid(ba)
    g = -jnp.exp(alog_ref[...]) * _softplus(ba + dtb_ref[...])
    bg_o[...] = jnp.where(lane < DN_HEADS, beta, g)
    if fused:
        ya_o, yc_o = rest[5:7]
        t = pl.program_id(0) % tiles_per_seq
        ya_o[...], yc_o[...] = _poolconf_tile(t, pool_o, glu_o, *pc_refs, *rest[7:], tt=tm, sub=64)
    qkv_o[...] = seg(OFF_QKV, OFF_Z)
    z_o[...] = seg(OFF_Z, OFF_B)


def _mod_spec(mode, chunk, tm, rows_per_batch):
    if mode == "batch":
        per = rows_per_batch // tm
        return pl.BlockSpec((None, 1, D_MODEL), lambda i: ((i // per) * 6 + chunk, 0, 0))
    return pl.BlockSpec((tm, D_MODEL), lambda i: (i, chunk))


def _inproj(x, mod, mode, layer, rows_per_batch, g1, w_in_r, alog_row, dtb_row, tm, poolconf_weights=None):
    n = x.shape[0]
    row = lambda w: pl.BlockSpec((tm, w), lambda i: (i, 0))
    const3 = lambda a, b: pl.BlockSpec((None, a, b), lambda i: (layer, 0, 0))
    outs = [POOL_WIDTH, 3 * DN_WIDTH, DN_WIDTH, CONF_WIDTH, LANES]
    in_specs = [row(D_MODEL),
                _mod_spec(mode, 0, tm, rows_per_batch),
                _mod_spec(mode, 1, tm, rows_per_batch),
                const3(1, D_MODEL),
                pl.BlockSpec((None, D_MODEL, N_IN), lambda i: (layer, 0, 0), pipeline_mode=pl.Buffered(1)),
                const3(1, LANES), const3(1, LANES)]
    args = (x, mod, mod, g1, w_in_r, alog_row, dtb_row)
    if poolconf_weights is None:
        tiles_per_seq, scratch, sem = None, [], "parallel"
    else:
        tiles_per_seq, scratch, sem = rows_per_batch // tm, _poolconf_scratch(tm), "arbitrary"
        in_specs += [const3(POOL_WIDTH, POOL_WIDTH), const3(1, POOL_WIDTH),
                     const3(CONF_K, CONF_WIDTH), const3(1, CONF_WIDTH),
                     const3(1, CONF_WIDTH), const3(1, CONF_WIDTH), const3(CONF_WIDTH, CONF_WIDTH),
                     pl.BlockSpec((CONF_WIDTH, CONF_WIDTH), lambda i: (0, 0))]
        args += tuple(poolconf_weights)
        outs += [POOL_WIDTH, CONF_WIDTH]
    return pl.pallas_call(
        functools.partial(_inproj_kernel, tm=tm, tiles_per_seq=tiles_per_seq),
        grid=(n // tm,),
        in_specs=in_specs,
        out_specs=[row(w) for w in outs],
        out_shape=[jax.ShapeDtypeStruct((n, w), F32) for w in outs],
        scratch_shapes=scratch,
        compiler_params=_cparams((sem,)),
        name="in_proj",
    )(*args)


def _pool_delta(win, u, pos, half):
    w_lo, w_hi = POOL_WINDOWS[2 * half], POOL_WINDOWS[2 * half + 1]
    a = win(0)
    for j in range(1, w_lo):
        a = a + win(j)
    b = win(w_lo)
    for j in range(w_lo + 1, w_hi):
        b = b + win(j)
    lane = lax.broadcasted_iota(jnp.int32, u.shape, 1)
    upper = lane >= LANES // 2
    s = a + jnp.where(upper, b, 0.0)
    cnt = jnp.where(upper, jnp.minimum(pos + 1, w_hi), jnp.minimum(pos + 1, w_lo)).astype(F32)
    return s / cnt - u


def _conf_post(dc, avg, lng, lnb, cpw):
    mu = _split_dot(dc, avg)
    xc = dc - mu
    var = _split_dot(xc * xc, avg)
    dn = xc * lax.rsqrt(var + EPS) * lng + lnb
    return _dot(_silu(dn), cpw)


def _poolconf_tile(t, u_ref, glu_ref, pw_ref, ps_ref, cw_ref, cb_ref, lng_ref, lnb_ref, cpw_ref,
                   avg_ref, epool, econf, pa, pb, erot, *, tt, sub):
    n = POOL_HIST + tt

    @pl.when(t == 0)
    def _():
        epool[0:POOL_HIST, :] = jnp.zeros((POOL_HIST, POOL_WIDTH), F32)
        econf[0:CONF_HIST, :] = jnp.zeros((CONF_HIST, CONF_WIDTH), F32)

    @pl.when(t > 0)
    def _():
        epool[0:POOL_HIST, :] = epool[tt:tt + POOL_HIST, :]
        econf[0:CONF_HIST, :] = econf[tt:tt + CONF_HIST, :]

    epool[POOL_HIST:, :] = u_ref[...]
    econf[CONF_HIST:, :] = glu_ref[...]

    pos = t * tt + lax.broadcasted_iota(jnp.int32, (tt, 1), 0)
    upper = lax.broadcasted_iota(jnp.int32, (tt, LANES), 1) >= LANES // 2
    halves = []
    for half in range(2):
        cols = slice(half * LANES, (half + 1) * LANES)
        w_lo, w_hi = POOL_WINDOWS[2 * half], POOL_WINDOWS[2 * half + 1]
        pa[8:n, :] = epool[8:n, cols] + epool[7:n - 1, cols]
        pb[16:n, :] = pa[16:n, :] + pa[14:n - 2, :]
        if half == 0:
            s_lo, s_hi = pa[POOL_HIST:n, :], pb[POOL_HIST:n, :]
        else:
            pa[24:n, :] = pb[24:n, :] + pb[20:n - 4, :]
            s_lo = pa[POOL_HIST:n, :]
            s_hi = s_lo + pa[POOL_HIST - 8:n - 8, :]
        cnt = jnp.where(upper, jnp.minimum(pos + 1, w_hi), jnp.minimum(pos + 1, w_lo)).astype(F32)
        halves.append(jnp.where(upper, s_hi, s_lo) / cnt - epool[POOL_HIST:n, cols])
    d = jnp.concatenate(halves, axis=1)
    ya = _dot(d, pw_ref[...]) * ps_ref[...]

    m = CONF_HIST + tt - 8
    for r in range(1, 8):
        erot[r - 1, 0:m, :] = econf[r:r + m, :]
    first = CONF_HIST - (CONF_K - 1)
    dc_blocks = []
    for r0 in range(0, tt, sub):
        acc = cb_ref[...]
        for j in range(CONF_K):
            a8, r = divmod(first + j, 8)
            lo = 8 * a8 + r0
            rows = econf[lo:lo + sub, :] if r == 0 else erot[r - 1, lo:lo + sub, :]
            acc = acc + cw_ref[j:j + 1, :] * rows
        dc_blocks.append(acc)
    dc = jnp.concatenate(dc_blocks, axis=0)
    return ya, _conf_post(dc, avg_ref[...], lng_ref[...], lnb_ref[...], cpw_ref[...])


def _poolconf_scratch(tt):
    return [pltpu.VMEM((POOL_HIST + tt, POOL_WIDTH), F32),
            pltpu.VMEM((CONF_HIST + tt, CONF_WIDTH), F32),
            pltpu.VMEM((POOL_HIST + tt, LANES), F32),
            pltpu.VMEM((POOL_HIST + tt, LANES), F32),
            pltpu.VMEM((7, CONF_HIST + tt, CONF_WIDTH), F32)]


def _poolconf_step_kernel(u_ref, sp_ref, glu_ref, sc_ref, pw_ref, ps_ref, cw_ref, cb_ref, lng_ref,
                          lnb_ref, cpw_ref, avg_ref, *rest, pos0):
    ya_o, yc_o, np_o, nc_o = rest[-4:]
    u = u_ref[...]
    rows = u.shape[0]
    pos = jnp.full((rows, 1), pos0, jnp.int32)
    halves = []
    for half in range(2):
        def win(j, half=half):
            if j == 0:
                return u[:, half * LANES:(half + 1) * LANES]
            lo = (POOL_BUF - j) * POOL_WIDTH + half * LANES
            return sp_ref[:, lo:lo + LANES]
        halves.append(_pool_delta(win, win(0), pos, half))
    d = jnp.concatenate(halves, axis=1)
    ya_o[...] = _dot(d, pw_ref[...]) * ps_ref[...]
    np_o[:, 0:(POOL_BUF - 1) * POOL_WIDTH] = sp_ref[:, POOL_WIDTH:]
    np_o[:, (POOL_BUF - 1) * POOL_WIDTH:] = u

    glu = glu_ref[...]
    acc = cb_ref[...] + cw_ref[CONF_K - 1:CONF_K, :] * glu
    for j in range(CONF_K - 1):
        acc = acc + cw_ref[j:j + 1, :] * sc_ref[:, j * CONF_WIDTH:(j + 1) * CONF_WIDTH]
    yc_o[...] = _conf_post(acc, avg_ref[...], lng_ref[...], lnb_ref[...], cpw_ref[...])
    nc_o[:, 0:(CONF_K - 2) * CONF_WIDTH] = sc_ref[:, CONF_WIDTH:]
    nc_o[:, (CONF_K - 2) * CONF_WIDTH:] = glu


def _poolconf_step(u, sp_all, glu, sc_all, prev, layer, pw_bd, pscale, cw, cb, lng, lnb, cpw, avg, bt):
    n = u.shape[0]
    row = lambda w: pl.BlockSpec((bt, w), lambda i: (i, 0))
    const3 = lambda a, b_: pl.BlockSpec((None, a, b_), lambda i: (layer, 0, 0))
    state = lambda a: pl.BlockSpec((None, bt, a.shape[2]), lambda i: (layer, i, 0))
    carry_specs, aliases = _carried(prev, 12, 2)
    return pl.pallas_call(
        functools.partial(_poolconf_step_kernel, pos0=PAST_LEN),
        grid=(n // bt,),
        in_specs=[row(POOL_WIDTH), state(sp_all), row(CONF_WIDTH), state(sc_all),
                  const3(POOL_WIDTH, POOL_WIDTH), const3(1, POOL_WIDTH),
                  const3(CONF_K, CONF_WIDTH), const3(1, CONF_WIDTH),
                  const3(1, CONF_WIDTH), const3(1, CONF_WIDTH),
                  const3(CONF_WIDTH, CONF_WIDTH),
                  pl.BlockSpec((CONF_WIDTH, CONF_WIDTH), lambda i: (0, 0))] + carry_specs,
        out_specs=[row(POOL_WIDTH), row(CONF_WIDTH), state(sp_all), state(sc_all)],
        out_shape=[jax.ShapeDtypeStruct((n, POOL_WIDTH), F32), jax.ShapeDtypeStruct((n, CONF_WIDTH), F32),
                   jax.ShapeDtypeStruct(sp_all.shape, F32), jax.ShapeDtypeStruct(sc_all.shape, F32)],
        input_output_aliases=aliases,
        compiler_params=_cparams(("parallel",)),
        name="pool_conf_step",
    )(u, sp_all, glu, sc_all, pw_bd, pscale, cw, cb, lng, lnb, cpw, avg, *(prev or ()))


def _l2norm(x):
    return x * lax.rsqrt(jnp.sum(x * x, axis=-1, keepdims=True) + EPS)


def _gated_norm(o, ng, z):
    return o * lax.rsqrt(jnp.mean(o * o, axis=-1, keepdims=True) + EPS) * ng * _silu(z)


def _tri_masks(c):
    ri = lax.broadcasted_iota(jnp.int32, (c, c), 0)
    ci = lax.broadcasted_iota(jnp.int32, (c, c), 1)
    masks = [ri // SOLVE_BASE == ci // SOLVE_BASE]
    size = SOLVE_BASE
    while size < c:
        masks.append((ri // (2 * size) == ci // (2 * size)) & (ri // size != ci // size))
        size *= 2
    return masks


def _tri_solve(bs, rhss, c, masks):
    ns = [jnp.where(masks[0], b, 0.0) for b in bs]
    ps = [_dot(n, n) for n in ns]
    power = 2
    while power < SOLVE_BASE:
        if 2 * power < SOLVE_BASE:
            sts = [_dot(jnp.concatenate([n, p], axis=0), p) for n, p in zip(ns, ps)]
            ns = [n + p + st[:c] for n, p, st in zip(ns, ps, sts)]
            ps = [st[c:] for st in sts]
        else:
            ns = [n + p + _dot(n, p) for n, p in zip(ns, ps)]
        power *= 2
    for m in masks[1:]:
        ls = [jnp.where(m, b, 0.0) for b in bs]
        tls = [l + _dot(n, l) for n, l in zip(ns, ls)]
        ns = [n + tl + _dot(tl, n) for n, tl in zip(ns, tls)]
    return [rhs + _dot(n, rhs) for n, rhs in zip(ns, rhss)]


def _gdn_kernel(qkv_ref, z_ref, bg_ref, cw_ref, ng_ref, ltri_ref, selb_ref, selg_ref, y_o, s_o,
                eq, eqs, s_scr, *, nb, tt, chunk):
    t = pl.program_id(1)
    width = 3 * DN_WIDTH
    dk = DN_HEAD_DIM

    @pl.when(t == 0)
    def _():
        eq[:, 0:QKV_HIST, :] = jnp.zeros((nb, QKV_HIST, width), F32)
        s_scr[...] = jnp.zeros(s_scr.shape, F32)

    @pl.when(t > 0)
    def _():
        eq[:, 0:QKV_HIST, :] = eq[:, tt:tt + QKV_HIST, :]

    eq[:, QKV_HIST:, :] = qkv_ref[...]

    for r in range(1, DN_CONV):
        eqs[:, r - 1] = eq[:, QKV_HIST - r:QKV_HIST - r + tt, :]

    def conv_cols(bi, lo):
        acc = cw_ref[DN_CONV - 1:DN_CONV, lo:lo + LANES] * eq[bi, QKV_HIST:QKV_HIST + tt, lo:lo + LANES]
        for r in range(1, DN_CONV):
            acc = acc + cw_ref[DN_CONV - 1 - r:DN_CONV - r, lo:lo + LANES] * eqs[bi, r - 1, :, lo:lo + LANES]
        return _silu(acc)

    def split3(x):
        hi = x.astype(BF16)
        r = x - hi.astype(F32)
        mid = r.astype(BF16)
        return hi, mid, (r - mid.astype(F32)).astype(BF16)

    def dot3(a, pieces, left):
        return sum(jnp.dot(a, p, preferred_element_type=F32) if left else
                   jnp.dot(p, a, preferred_element_type=F32) for p in pieces)

    def l2n(x):
        return x * lax.rsqrt(_lane_sum(x * x) + EPS)

    rc = lax.broadcasted_iota(jnp.int32, (chunk, chunk), 0)
    cc = lax.broadcasted_iota(jnp.int32, (chunk, chunk), 1)
    incl = rc >= cc
    strict = rc > cc
    masks = _tri_masks(chunk)
    chunks = range(tt // chunk)
    rows = [slice(c * chunk, (c + 1) * chunk) for c in chunks]

    seqs = [(bi, h) for bi in range(nb) for h in range(DN_HEADS)]
    q, k, v, beta, gcol, grow, eg = [], [], [], [], [], [], []
    for bi in range(nb):
        bg3 = split3(bg_ref[bi])
        gc = dot3(ltri_ref[...], bg3, True)
        gt = gc.T
        beta_d = dot3(selb_ref[...], bg3, False)
        g_d = dot3(selg_ref[...], split3(gc), False)
        eg_d = jnp.exp(g_d)
        for h in range(DN_HEADS):
            cols = slice(h * dk, (h + 1) * dk)
            q.append(l2n(conv_cols(bi, h * dk)) * (dk ** -0.5))
            k.append(l2n(conv_cols(bi, DN_WIDTH + h * dk)))
            v.append(conv_cols(bi, 2 * DN_WIDTH + h * dk))
            beta.append(beta_d[:, cols])
            gcol.append(g_d[:, cols])
            eg.append(eg_d[:, cols])
            grow.append(gt[DN_HEADS + h:DN_HEADS + h + 1, :])
    ns = range(len(seqs))
    qg = [q[s] * eg[s] for s in ns]
    rhs_all = [jnp.concatenate([v[s] * beta[s], k[s] * (beta[s] * eg[s])], axis=1) for s in ns]

    pairs = [(s, c) for s in ns for c in chunks]
    kq = [_dot(jnp.concatenate([k[s][rows[c]], q[s][rows[c]]], axis=0), k[s][rows[c]].T)
          for s, c in pairs]
    decay = [jnp.where(incl, jnp.exp(jnp.where(incl, gcol[s][rows[c]] - grow[s][:, rows[c]], 0.0)), 0.0)
             for s, c in pairs]
    bmat = [jnp.where(strict, -(beta[s][rows[c]] * kq[i][:chunk] * decay[i]), 0.0)
            for i, (s, c) in enumerate(pairs)]
    qk = [kq[i][chunk:] * decay[i] for i in range(len(pairs))]
    sol = _tri_solve(bmat, [rhs_all[s][rows[c]] for s, c in pairs], chunk, masks)
    g_last = [gcol[s][(c + 1) * chunk - 1:(c + 1) * chunk] for s, c in pairs]
    k_tail = [k[s][rows[c]] * jnp.exp(g_last[i] - gcol[s][rows[c]]) for i, (s, c) in enumerate(pairs)]

    state = [s_scr[bi, h] for bi, h in seqs]
    o = [[] for _ in ns]
    for c in chunks:
        idx = [pairs.index((s, c)) for s in ns]
        ws = [_dot(jnp.concatenate([sol[idx[s]][:, dk:], qg[s][rows[c]]], axis=0), state[s]) for s in ns]
        u = [sol[idx[s]][:, :dk] - ws[s][:chunk] for s in ns]
        for s in ns:
            o[s].append(ws[s][chunk:] + _dot(qk[idx[s]], u[s]))
        state = [state[s] * jnp.exp(g_last[idx[s]]) + _dot(k_tail[idx[s]].T, u[s]) for s in ns]
    for s, (bi, h) in enumerate(seqs):
        s_scr[bi, h] = state[s]
        os_ = jnp.concatenate(o[s], axis=0)
        rms = lax.rsqrt(_lane_sum(os_ * os_) * (1.0 / dk) + EPS)
        y_o[bi, :, h * dk:(h + 1) * dk] = os_ * rms * ng_ref[...] * _silu(z_ref[bi, :, h * dk:(h + 1) * dk])

    @pl.when(t == pl.num_programs(1) - 1)
    def _():
        s_o[...] = s_scr[...]


def _gdn(qkv, z, bg, layer, batch, seq, cw, ng, consts, nb, tt):
    assert GDN_CHUNK == LANES and DN_HEAD_DIM == LANES
    blk = lambda w: pl.BlockSpec((nb, tt, w), lambda p, t: (p, t, 0))
    const3 = lambda a, b_: pl.BlockSpec((None, a, b_), lambda p, t: (layer, 0, 0))
    const2 = lambda a, b_: pl.BlockSpec((a, b_), lambda p, t: (0, 0))
    state = (DN_HEADS, DN_HEAD_DIM, DN_HEAD_DIM)
    seq3 = lambda a: a.reshape(batch, seq, a.shape[-1])
    y, s_new = pl.pallas_call(
        functools.partial(_gdn_kernel, nb=nb, tt=tt, chunk=GDN_CHUNK),
        grid=(batch // nb, seq // tt),
        in_specs=[blk(3 * DN_WIDTH), blk(DN_WIDTH), blk(LANES),
                  const3(DN_CONV, 3 * DN_WIDTH), const3(1, DN_HEAD_DIM),
                  const2(tt, tt), const2(LANES, DN_WIDTH), const2(LANES, DN_WIDTH)],
        out_specs=[blk(DN_WIDTH),
                   pl.BlockSpec((nb,) + state, lambda p, t: (p, 0, 0, 0))],
        out_shape=[jax.ShapeDtypeStruct((batch, seq, DN_WIDTH), F32),
                   jax.ShapeDtypeStruct((batch,) + state, F32)],
        scratch_shapes=[pltpu.VMEM((nb, QKV_HIST + tt, 3 * DN_WIDTH), F32),
                        pltpu.VMEM((nb, DN_CONV - 1, tt, 3 * DN_WIDTH), F32),
                        pltpu.VMEM((nb,) + state, F32)],
        compiler_params=_cparams(("parallel", "arbitrary")),
        name="gated_delta",
    )(seq3(qkv), seq3(z), seq3(bg), cw, ng, *consts)
    return y.reshape(batch * seq, DN_WIDTH), s_new


def _gdn_step_kernel(qkv_ref, sq_ref, z_ref, bg_ref, s_ref, cw_ref, ng_ref, *rest):
    y_o, nq_o, s_o = rest[-3:]
    width = 3 * DN_WIDTH
    qkv = qkv_ref[...]
    bt = qkv.shape[0]
    acc = cw_ref[DN_CONV - 1:DN_CONV, :] * qkv
    for j in range(DN_CONV - 1):
        acc = acc + cw_ref[j:j + 1, :] * sq_ref[:, j * width:(j + 1) * width]
    act = _silu(acc)
    nq_o[:, 0:(DN_CONV - 2) * width] = sq_ref[:, width:]
    nq_o[:, (DN_CONV - 2) * width:] = qkv
    bg = bg_ref[...]
    for h in range(DN_HEADS):
        cols = slice(h * DN_HEAD_DIM, (h + 1) * DN_HEAD_DIM)
        q = _l2norm(act[:, cols]) * (DN_HEAD_DIM ** -0.5)
        k = _l2norm(act[:, DN_WIDTH + h * DN_HEAD_DIM:DN_WIDTH + (h + 1) * DN_HEAD_DIM])
        v = act[:, 2 * DN_WIDTH + h * DN_HEAD_DIM:2 * DN_WIDTH + (h + 1) * DN_HEAD_DIM]
        beta = bg[:, h:h + 1]
        eg = jnp.exp(bg[:, DN_HEADS + h:DN_HEADS + h + 1])
        qk = jnp.sum(q * k, axis=-1, keepdims=True)
        outs = []
        for b in range(bt):
            s = s_ref[b, h]
            k_col = jnp.broadcast_to(k[b:b + 1], (DN_HEAD_DIM, DN_HEAD_DIM)).T
            q_col = jnp.broadcast_to(q[b:b + 1], (DN_HEAD_DIM, DN_HEAD_DIM)).T
            ks = jnp.sum(k_col * s, axis=0, keepdims=True)
            qs = jnp.sum(q_col * s, axis=0, keepdims=True)
            e = eg[b:b + 1]
            u = beta[b:b + 1] * (v[b:b + 1] - e * ks)
            outs.append(e * qs + qk[b:b + 1] * u)
            s_o[b, h] = e * s + k_col * u
        o = jnp.concatenate(outs, axis=0)
        y_o[:, cols] = _gated_norm(o, ng_ref[...], z_ref[:, cols])


def _carried(prev, first_input, first_output):
    if prev is None:
        return [], {}
    specs = [pl.BlockSpec(memory_space=pl.ANY) for _ in prev]
    return specs, {first_input + j: first_output + j for j in range(len(prev))}


def _gdn_step(qkv, sq_all, z, bg, s_all, prev, layer, cw, ng, bt):
    n = qkv.shape[0]
    row = lambda w: pl.BlockSpec((bt, w), lambda i: (i, 0))
    const3 = lambda a, b_: pl.BlockSpec((None, a, b_), lambda i: (layer, 0, 0))
    sblk = pl.BlockSpec((None, bt, DN_HEADS, DN_HEAD_DIM, DN_HEAD_DIM), lambda i: (layer, i, 0, 0, 0))
    qblk = pl.BlockSpec((None, bt, sq_all.shape[2]), lambda i: (layer, i, 0))
    carry_specs, aliases = _carried(prev, 7, 1)
    return pl.pallas_call(
        _gdn_step_kernel,
        grid=(n // bt,),
        in_specs=[row(3 * DN_WIDTH), qblk, row(DN_WIDTH), row(LANES), sblk,
                  const3(DN_CONV, 3 * DN_WIDTH), const3(1, DN_HEAD_DIM)] + carry_specs,
        out_specs=[row(DN_WIDTH), qblk, sblk],
        out_shape=[jax.ShapeDtypeStruct(z.shape, F32), jax.ShapeDtypeStruct(sq_all.shape, F32),
                   jax.ShapeDtypeStruct(s_all.shape, F32)],
        input_output_aliases=aliases,
        compiler_params=_cparams(("parallel",)),
        name="gated_delta_step",
    )(qkv, sq_all, z, bg, s_all, cw, ng, *(prev or ()))


def _mix_ffn_kernel(x_ref, ya_ref, yb_ref, yc_ref, gate1_ref, shift_ref, scale_ref, gate_ref, g_ref,
                    wo_ref, w1_ref, w2_ref, gf_ref, o_ref, *, final, hidden_chunk):
    a_end, b_end = POOL_WIDTH, POOL_WIDTH + DN_WIDTH
    mix = (_dot(ya_ref[...], wo_ref[0:a_end, :]) + _dot(yb_ref[...], wo_ref[a_end:b_end, :])
           + _dot(yc_ref[...], wo_ref[b_end:, :]))
    x = x_ref[...] + gate1_ref[...] * mix
    h = _mod_rmsnorm(x, g_ref[...], scale_ref[...], shift_ref[...]).astype(BF16)
    acc = jnp.zeros(x.shape, F32)
    for c in range(0, D_FF, hidden_chunk):
        a = jnp.maximum(jnp.dot(h, w1_ref[:, c:c + hidden_chunk], preferred_element_type=F32), 0.0)
        acc = acc + jnp.dot((a * a).astype(BF16), w2_ref[c:c + hidden_chunk, :],
                            preferred_element_type=F32)
    y = x + gate_ref[...] * acc
    if final:
        y = y * lax.rsqrt(jnp.mean(y * y, axis=-1, keepdims=True) + EPS) * gf_ref[...]
    o_ref[...] = y


def _mix_ffn(x, ya, yb, yc, mod, mode, layer, rows_per_batch, g2, w_out, w1, w2, g_final, final, tm):
    n = x.shape[0]
    row = lambda w: pl.BlockSpec((tm, w), lambda i: (i, 0))
    resident = lambda a, b: pl.BlockSpec((None, a, b), lambda i: (layer, 0, 0),
                                         pipeline_mode=pl.Buffered(1))
    return pl.pallas_call(
        functools.partial(_mix_ffn_kernel, final=final, hidden_chunk=1024),
        grid=(n // tm,),
        in_specs=[row(D_MODEL), row(POOL_WIDTH), row(DN_WIDTH), row(CONF_WIDTH),
                  _mod_spec(mode, 2, tm, rows_per_batch),
                  _mod_spec(mode, 3, tm, rows_per_batch),
                  _mod_spec(mode, 4, tm, rows_per_batch),
                  _mod_spec(mode, 5, tm, rows_per_batch),
                  pl.BlockSpec((None, 1, D_MODEL), lambda i: (layer, 0, 0)),
                  resident(D_MODEL, D_MODEL), resident(D_MODEL, D_FF), resident(D_FF, D_MODEL),
                  pl.BlockSpec((1, D_MODEL), lambda i: (0, 0))],
        out_specs=row(D_MODEL),
        out_shape=jax.ShapeDtypeStruct(x.shape, F32),
        compiler_params=_cparams(("parallel",)),
        name="mix_ffn",
    )(x, ya, yb, yc, mod, mod, mod, mod, g2, w_out, w1, w2, g_final)


def kernel(x_prompt, x_sample, state_pool, state_qkv_conv, state_delta, state_conv, c_prompt, c_sample,
           w_ada, b_ada, g_norm1, g_norm2, w_in, pool_w, pool_scale, qkv_conv_w, a_log, dt_bias,
           dn_norm_g, conf_dw_w, conf_dw_b, conf_ln_g, conf_ln_b, conf_pw_w, w_out, w_ff1, w_ff2, g_final):
    depth = w_in.shape[0]
    bp, seq, d = x_prompt.shape
    bs = x_sample.shape[0]
    assert d == D_MODEL and x_sample.shape[1] == 1

    w_in_r = w_in.astype(BF16)
    groups = pool_w.shape[1]
    pw_bd = jnp.einsum("lgcd,gh->lgchd", pool_w, jnp.eye(groups, dtype=pool_w.dtype))
    pw_bd = pw_bd.reshape(depth, POOL_WIDTH, POOL_WIDTH).astype(BF16)
    head_of = jnp.arange(CONF_WIDTH) // (CONF_WIDTH // CONF_HEADS)
    avg = jnp.where(head_of[:, None] == head_of[None, :], CONF_HEADS / CONF_WIDTH, 0.0).astype(BF16)
    tpos = jnp.arange(GDN_TILE)
    ltri = ((tpos[:, None] >= tpos[None, :])
            & (tpos[:, None] // GDN_CHUNK == tpos[None, :] // GDN_CHUNK)).astype(BF16)
    lane_id = jnp.arange(LANES)[:, None]
    head_id = jnp.arange(DN_WIDTH)[None, :] // DN_HEAD_DIM
    gdn_consts = (ltri, (lane_id == head_id).astype(BF16), (lane_id == DN_HEADS + head_id).astype(BF16))
    lane_pad = lambda a: jnp.pad(a, ((0, 0), (DN_HEADS, LANES - 2 * DN_HEADS)))[:, None, :]
    alog_row, dtb_row = lane_pad(a_log), lane_pad(dt_bias)
    r3 = lambda a: a[:, None, :]
    g1, g2, pscale, ng = r3(g_norm1), r3(g_norm2), r3(pool_scale), r3(dn_norm_g)
    cb, lng, lnb = r3(conf_dw_b), r3(conf_ln_g), r3(conf_ln_b)
    cpw = conf_pw_w.astype(BF16)
    w_out_b, w1_b, w2_b = w_out.astype(BF16), w_ff1.astype(BF16), w_ff2.astype(BF16)
    gf = g_final[None, :]

    mods = _ada(jnp.concatenate([c_prompt, c_sample], axis=0), w_ada, b_ada)
    mods_p = mods[:, :bp].reshape(depth, bp * 6, 1, d)
    mods_s = mods[:, bp:]

    xp = x_prompt.reshape(bp * seq, d)
    xs = x_sample.reshape(bs, d)
    tm_p, nb_gdn, tt_gdn, bt_mix, bt_gdn = 512, GDN_SEQS, GDN_TILE, min(32, bs), min(16, bs)
    pc_weights = (pw_bd, pscale, conf_dw_w, cb, lng, lnb, cpw, avg)
    outs = {0: [], 2: [], 4: [], 6: []}
    sp_all = state_pool.reshape(depth, bs, -1)
    sc_all = state_conv.reshape(depth, bs, -1)
    sq_all = state_qkv_conv.reshape(depth, bs, -1)
    pc_carry = gdn_carry = None
    for l in range(depth):
        last = l == depth - 1
        mp = mods_p[l]
        u, qkv, z, glu, bg, ya, yc = _inproj(xp, mp, "batch", l, seq, g1, w_in_r, alog_row, dtb_row, tm_p,
                                             poolconf_weights=pc_weights)
        yb, s_new = _gdn(qkv, z, bg, l, bp, seq, qkv_conv_w, ng, gdn_consts, nb_gdn, tt_gdn)
        xp = _mix_ffn(xp, ya, yb, yc, mp, "batch", l, seq, g2, w_out_b, w1_b, w2_b, gf, last, tm_p)
        outs[0].append(u.reshape(bp, seq, -1)[:, seq - POOL_BUF:])
        outs[2].append(qkv.reshape(bp, seq, -1)[:, seq - (DN_CONV - 1):])
        outs[4].append(s_new)
        outs[6].append(glu.reshape(bp, seq, -1)[:, seq - (CONF_K - 1):])
        ms = mods_s[l]
        u, qkv, z, glu, bg = _inproj(xs, ms, "row", l, 1, g1, w_in_r, alog_row, dtb_row, bs)
        ya, yc, *pc_carry = _poolconf_step(u, sp_all, glu, sc_all, pc_carry, l, *pc_weights, bt_mix)
        yb, *gdn_carry = _gdn_step(qkv, sq_all, z, bg, state_delta, gdn_carry, l, qkv_conv_w, ng, bt_gdn)
        xs = _mix_ffn(xs, ya, yb, yc, ms, "row", l, 1, g2, w_out_b, w1_b, w2_b, gf, last, bs)
    n_pool, n_conf = pc_carry
    n_qkv, n_delta = gdn_carry
    return (xp.reshape(bp, seq, d), xs.reshape(bs, 1, d),
            jnp.stack(outs[0]), n_pool.reshape(state_pool.shape),
            jnp.stack(outs[2]), n_qkv.reshape(state_qkv_conv.shape),
            jnp.stack(outs[4]), n_delta,
            jnp.stack(outs[6]), n_conf.reshape(state_conv.shape))
```

```python
import functools

import jax
import jax.numpy as jnp
from jax import lax
from jax.experimental import pallas as pl
from jax.experimental.pallas import tpu as pltpu

F32 = jnp.float32
BF16 = jnp.bfloat16
EPS = 1e-6

D_MODEL = 1024
POOL_WIDTH = 256
POOL_WINDOWS = (2, 4, 8, 16)
POOL_BUF = 15
DN_WIDTH = 512
DN_HEAD_DIM = 128
DN_HEADS = 4
DN_CONV = 4
CONF_WIDTH = 256
CONF_HEADS = 4
CONF_K = 31
D_FF = 4 * D_MODEL
PAST_LEN = 16384
OFF_QKV = POOL_WIDTH
OFF_Z = OFF_QKV + 3 * DN_WIDTH
OFF_B = OFF_Z + DN_WIDTH
OFF_A = OFF_B + DN_HEADS
OFF_GLU = OFF_A + DN_HEADS
N_IN = OFF_GLU + 2 * CONF_WIDTH

LANES = 128
POOL_HIST = 32
CONF_HIST = 32
QKV_HIST = 8
GDN_CHUNK = 128
GDN_TILE = 256
GDN_SEQS = 2
SOLVE_BASE = 16
VMEM_LIMIT = 56 * 1024 * 1024


def _cparams(sem):
    return pltpu.CompilerParams(dimension_semantics=sem, vmem_limit_bytes=VMEM_LIMIT)


def _dot(a, b):
    return jnp.dot(a.astype(BF16), b.astype(BF16), preferred_element_type=F32)


def _lane_sum(x):
    ones = jnp.ones((x.shape[1], LANES), BF16)
    return jnp.dot(x.astype(BF16), ones, preferred_element_type=F32)


def _split_dot(x, m):
    hi = x.astype(BF16)
    lo = (x - hi.astype(F32)).astype(BF16)
    return (jnp.dot(hi, m, preferred_element_type=F32)
            + jnp.dot(lo, m, preferred_element_type=F32))


def _sigmoid(x):
    return 1.0 / (1.0 + jnp.exp(-x))


def _silu(x):
    return x * _sigmoid(x)


def _softplus(x):
    return jnp.maximum(x, 0.0) + jnp.log1p(jnp.exp(-jnp.abs(x)))


def _mod_rmsnorm(x, g, scale, shift):
    y = x * lax.rsqrt(jnp.mean(x * x, axis=-1, keepdims=True) + EPS)
    return (y * g) * (1.0 + scale) + shift


def _ada_kernel(c_ref, w_ref, b_ref, o_ref):
    o_ref[...] = _dot(_silu(c_ref[...]), w_ref[...]) + b_ref[...]


def _ada(c_all, w_ada, b_ada):
    depth, d, n = w_ada.shape
    rows = c_all.shape[0]
    tn = 1536
    return pl.pallas_call(
        _ada_kernel,
        grid=(depth, n // tn),
        in_specs=[pl.BlockSpec((rows, d), lambda l, j: (0, 0)),
                  pl.BlockSpec((None, d, tn), lambda l, j: (l, 0, j)),
                  pl.BlockSpec((None, 1, tn), lambda l, j: (l, 0, j))],
        out_specs=pl.BlockSpec((None, rows, tn), lambda l, j: (l, 0, j)),
        out_shape=jax.ShapeDtypeStruct((depth, rows, n), F32),
        compiler_params=_cparams(("arbitrary", "arbitrary")),
        name="ada_mod",
    )(c_all, w_ada, b_ada.reshape(depth, 1, n))


N_POOLCONF_REFS = 8


def _inproj_kernel(x_ref, shift_ref, scale_ref, g_ref, w_ref, alog_ref, dtb_ref, *rest, tm, tiles_per_seq):
    fused = tiles_per_seq is not None
    pc_refs, rest = (rest[:N_POOLCONF_REFS], rest[N_POOLCONF_REFS:]) if fused else ((), rest)
    pool_o, qkv_o, z_o, glu_o, bg_o = rest[:5]
    h = _mod_rmsnorm(x_ref[...], g_ref[...], scale_ref[...], shift_ref[...]).astype(BF16)

    def seg(lo, hi):
        return jnp.dot(h, w_ref[:, lo:hi], preferred_element_type=F32)

    pool_o[...] = seg(0, OFF_QKV)
    tail = seg(OFF_B, N_IN)
    glu_lo = OFF_GLU - OFF_B
    glu_o[...] = (tail[:, glu_lo:glu_lo + CONF_WIDTH]
                  * _sigmoid(tail[:, glu_lo + CONF_WIDTH:glu_lo + 2 * CONF_WIDTH]))
    ba = tail[:, :LANES]
    lane = lax.broadcasted_iota(jnp.int32, ba.shape, 1)
    beta = _sigmoid(ba)
    g = -jnp.exp(alog_ref[...]) * _softplus(ba + dtb_ref[...])
    bg_o[...] = jnp.where(lane < DN_HEADS, beta, g)
    if fused:
        ya_o, yc_o = rest[5:7]
        t = pl.program_id(0) % tiles_per_seq
        ya_o[...], yc_o[...] = _poolconf_tile(t, pool_o, glu_o, *pc_refs, *rest[7:], tt=tm, sub=64)
    qkv_o[...] = seg(OFF_QKV, OFF_Z)
    z_o[...] = seg(OFF_Z, OFF_B)


def _mod_spec(mode, chunk, tm, rows_per_batch):
    if mode == "batch":
        per = rows_per_batch // tm
        return pl.BlockSpec((None, 1, D_MODEL), lambda i: ((i // per) * 6 + chunk, 0, 0))
    return pl.BlockSpec((tm, D_MODEL), lambda i: (i, chunk))


def _inproj(x, mod, mode, layer, rows_per_batch, g1, w_in_r, alog_row, dtb_row, tm, poolconf_weights=None):
    n = x.shape[0]
    row = lambda w: pl.BlockSpec((tm, w), lambda i: (i, 0))
    const3 = lambda a, b: pl.BlockSpec((None, a, b), lambda i: (layer, 0, 0))
    outs = [POOL_WIDTH, 3 * DN_WIDTH, DN_WIDTH, CONF_WIDTH, LANES]
    in_specs = [row(D_MODEL),
                _mod_spec(mode, 0, tm, rows_per_batch),
                _mod_spec(mode, 1, tm, rows_per_batch),
                const3(1, D_MODEL),
                pl.BlockSpec((None, D_MODEL, N_IN), lambda i: (layer, 0, 0), pipeline_mode=pl.Buffered(1)),
                const3(1, LANES), const3(1, LANES)]
    args = (x, mod, mod, g1, w_in_r, alog_row, dtb_row)
    if poolconf_weights is None:
        tiles_per_seq, scratch, sem = None, [], "parallel"
    else:
        tiles_per_seq, scratch, sem = rows_per_batch // tm, _poolconf_scratch(tm), "arbitrary"
        in_specs += [const3(POOL_WIDTH, POOL_WIDTH), const3(1, POOL_WIDTH),
                     const3(CONF_K, CONF_WIDTH), const3(1, CONF_WIDTH),
                     const3(1, CONF_WIDTH), const3(1, CONF_WIDTH), const3(CONF_WIDTH, CONF_WIDTH),
                     pl.BlockSpec((CONF_WIDTH, CONF_WIDTH), lambda i: (0, 0))]
        args += tuple(poolconf_weights)
        outs += [POOL_WIDTH, CONF_WIDTH]
    return pl.pallas_call(
        functools.partial(_inproj_kernel, tm=tm, tiles_per_seq=tiles_per_seq),
        grid=(n // tm,),
        in_specs=in_specs,
        out_specs=[row(w) for w in outs],
        out_shape=[jax.ShapeDtypeStruct((n, w), F32) for w in outs],
        scratch_shapes=scratch,
        compiler_params=_cparams((sem,)),
        name="in_proj",
    )(*args)


def _pool_delta(win, u, pos, half):
    w_lo, w_hi = POOL_WINDOWS[2 * half], POOL_WINDOWS[2 * half + 1]
    a = win(0)
    for j in range(1, w_lo):
        a = a + win(j)
    b = win(w_lo)
    for j in range(w_lo + 1, w_hi):
        b = b + win(j)
    lane = lax.broadcasted_iota(jnp.int32, u.shape, 1)
    upper = lane >= LANES // 2
    s = a + jnp.where(upper, b, 0.0)
    cnt = jnp.where(upper, jnp.minimum(pos + 1, w_hi), jnp.minimum(pos + 1, w_lo)).astype(F32)
    return s / cnt - u


def _conf_post(dc, avg, lng, lnb, cpw):
    mu = _split_dot(dc, avg)
    xc = dc - mu
    var = _split_dot(xc * xc, avg)
    dn = xc * lax.rsqrt(var + EPS) * lng + lnb
    return _dot(_silu(dn), cpw)


def _poolconf_tile(t, u_ref, glu_ref, pw_ref, ps_ref, cw_ref, cb_ref, lng_ref, lnb_ref, cpw_ref,
                   avg_ref, epool, econf, pa, pb, erot, *, tt, sub):
    n = POOL_HIST + tt

    @pl.when(t == 0)
    def _():
        epool[0:POOL_HIST, :] = jnp.zeros((POOL_HIST, POOL_WIDTH), F32)
        econf[0:CONF_HIST, :] = jnp.zeros((CONF_HIST, CONF_WIDTH), F32)

    @pl.when(t > 0)
    def _():
        epool[0:POOL_HIST, :] = epool[tt:tt + POOL_HIST, :]
        econf[0:CONF_HIST, :] = econf[tt:tt + CONF_HIST, :]

    epool[POOL_HIST:, :] = u_ref[...]
    econf[CONF_HIST:, :] = glu_ref[...]

    pos = t * tt + lax.broadcasted_iota(jnp.int32, (tt, 1), 0)
    upper = lax.broadcasted_iota(jnp.int32, (tt, LANES), 1) >= LANES // 2
    halves = []
    for half in range(2):
        cols = slice(half * LANES, (half + 1) * LANES)
        w_lo, w_hi = POOL_WINDOWS[2 * half], POOL_WINDOWS[2 * half + 1]
        pa[8:n, :] = epool[8:n, cols] + epool[7:n - 1, cols]
        pb[16:n, :] = pa[16:n, :] + pa[14:n - 2, :]
        if half == 0:
            s_lo, s_hi = pa[POOL_HIST:n, :], pb[POOL_HIST:n, :]
        else:
            pa[24:n, :] = pb[24:n, :] + pb[20:n - 4, :]
            s_lo = pa[POOL_HIST:n, :]
            s_hi = s_lo + pa[POOL_HIST - 8:n - 8, :]
        cnt = jnp.where(upper, jnp.minimum(pos + 1, w_hi), jnp.minimum(pos + 1, w_lo)).astype(F32)
        halves.append(jnp.where(upper, s_hi, s_lo) / cnt - epool[POOL_HIST:n, cols])
    d = jnp.concatenate(halves, axis=1)
    ya = _dot(d, pw_ref[...]) * ps_ref[...]

    m = CONF_HIST + tt - 8
    for r in range(1, 8):
        erot[r - 1, 0:m, :] = econf[r:r + m, :]
    first = CONF_HIST - (CONF_K - 1)
    dc_blocks = []
    for r0 in range(0, tt, sub):
        acc = cb_ref[...]
        for j in range(CONF_K):
            a8, r = divmod(first + j, 8)
            lo = 8 * a8 + r0
            rows = econf[lo:lo + sub, :] if r == 0 else erot[r - 1, lo:lo + sub, :]
            acc = acc + cw_ref[j:j + 1, :] * rows
        dc_blocks.append(acc)
    dc = jnp.concatenate(dc_blocks, axis=0)
    return ya, _conf_post(dc, avg_ref[...], lng_ref[...], lnb_ref[...], cpw_ref[...])


def _poolconf_scratch(tt):
    return [pltpu.VMEM((POOL_HIST + tt, POOL_WIDTH), F32),
            pltpu.VMEM((CONF_HIST + tt, CONF_WIDTH), F32),
            pltpu.VMEM((POOL_HIST + tt, LANES), F32),
            pltpu.VMEM((POOL_HIST + tt, LANES), F32),
            pltpu.VMEM((7, CONF_HIST + tt, CONF_WIDTH), F32)]


def _poolconf_step_kernel(u_ref, sp_ref, glu_ref, sc_ref, pw_ref, ps_ref, cw_ref, cb_ref, lng_ref,
                          lnb_ref, cpw_ref, avg_ref, *rest, pos0):
    ya_o, yc_o, np_o, nc_o = rest[-4:]
    u = u_ref[...]
    rows = u.shape[0]
    pos = jnp.full((rows, 1), pos0, jnp.int32)
    halves = []
    for half in range(2):
        def win(j, half=half):
            if j == 0:
                return u[:, half * LANES:(half + 1) * LANES]
            return sp_ref[:, POOL_BUF - j, half * LANES:(half + 1) * LANES]
        halves.append(_pool_delta(win, win(0), pos, half))
    d = jnp.concatenate(halves, axis=1)
    ya_o[...] = _dot(d, pw_ref[...]) * ps_ref[...]
    np_o[:, 0:POOL_BUF - 1, :] = sp_ref[:, 1:POOL_BUF, :]
    np_o[:, POOL_BUF - 1, :] = u

    glu = glu_ref[...]
    acc = cb_ref[...] + cw_ref[CONF_K - 1:CONF_K, :] * glu
    for j in range(CONF_K - 1):
        acc = acc + cw_ref[j:j + 1, :] * sc_ref[:, j, :]
    yc_o[...] = _conf_post(acc, avg_ref[...], lng_ref[...], lnb_ref[...], cpw_ref[...])
    nc_o[:, 0:CONF_K - 2, :] = sc_ref[:, 1:CONF_K - 1, :]
    nc_o[:, CONF_K - 2, :] = glu


def _poolconf_step(u, sp_all, glu, sc_all, prev, layer, pw_bd, pscale, cw, cb, lng, lnb, cpw, avg, bt):
    n = u.shape[0]
    row = lambda w: pl.BlockSpec((bt, w), lambda i: (i, 0))
    const3 = lambda a, b_: pl.BlockSpec((None, a, b_), lambda i: (layer, 0, 0))
    state = lambda a: pl.BlockSpec((None, bt) + a.shape[2:], lambda i: (layer, i, 0, 0))
    carry_specs, aliases = _carried(prev, 12, 2)
    return pl.pallas_call(
        functools.partial(_poolconf_step_kernel, pos0=PAST_LEN),
        grid=(n // bt,),
        in_specs=[row(POOL_WIDTH), state(sp_all), row(CONF_WIDTH), state(sc_all),
                  const3(POOL_WIDTH, POOL_WIDTH), const3(1, POOL_WIDTH),
                  const3(CONF_K, CONF_WIDTH), const3(1, CONF_WIDTH),
                  const3(1, CONF_WIDTH), const3(1, CONF_WIDTH),
                  const3(CONF_WIDTH, CONF_WIDTH),
                  pl.BlockSpec((CONF_WIDTH, CONF_WIDTH), lambda i: (0, 0))] + carry_specs,
        out_specs=[row(POOL_WIDTH), row(CONF_WIDTH), state(sp_all), state(sc_all)],
        out_shape=[jax.ShapeDtypeStruct((n, POOL_WIDTH), F32), jax.ShapeDtypeStruct((n, CONF_WIDTH), F32),
                   jax.ShapeDtypeStruct(sp_all.shape, F32), jax.ShapeDtypeStruct(sc_all.shape, F32)],
        input_output_aliases=aliases,
        compiler_params=_cparams(("parallel",)),
        name="pool_conf_step",
    )(u, sp_all, glu, sc_all, pw_bd, pscale, cw, cb, lng, lnb, cpw, avg, *prev)


def _l2norm(x):
    return x * lax.rsqrt(jnp.sum(x * x, axis=-1, keepdims=True) + EPS)


def _gated_norm(o, ng, z):
    return o * lax.rsqrt(jnp.mean(o * o, axis=-1, keepdims=True) + EPS) * ng * _silu(z)


def _tri_masks(c):
    ri = lax.broadcasted_iota(jnp.int32, (c, c), 0)
    ci = lax.broadcasted_iota(jnp.int32, (c, c), 1)
    masks = [ri // SOLVE_BASE == ci // SOLVE_BASE]
    size = SOLVE_BASE
    while size < c:
        masks.append((ri // (2 * size) == ci // (2 * size)) & (ri // size != ci // size))
        size *= 2
    return masks


def _tri_solve(bs, rhss, c, masks):
    ns = [jnp.where(masks[0], b, 0.0) for b in bs]
    ps = [_dot(n, n) for n in ns]
    power = 2
    while power < SOLVE_BASE:
        if 2 * power < SOLVE_BASE:
            sts = [_dot(jnp.concatenate([n, p], axis=0), p) for n, p in zip(ns, ps)]
            ns = [n + p + st[:c] for n, p, st in zip(ns, ps, sts)]
            ps = [st[c:] for st in sts]
        else:
            ns = [n + p + _dot(n, p) for n, p in zip(ns, ps)]
        power *= 2
    for m in masks[1:]:
        ls = [jnp.where(m, b, 0.0) for b in bs]
        tls = [l + _dot(n, l) for n, l in zip(ns, ls)]
        ns = [n + tl + _dot(tl, n) for n, tl in zip(ns, tls)]
    return [rhs + _dot(n, rhs) for n, rhs in zip(ns, rhss)]


def _gdn_kernel(qkv_ref, z_ref, bg_ref, cw_ref, ng_ref, ltri_ref, selb_ref, selg_ref, y_o, s_o,
                eq, eqs, s_scr, *, nb, tt, chunk):
    t = pl.program_id(1)
    width = 3 * DN_WIDTH
    dk = DN_HEAD_DIM

    @pl.when(t == 0)
    def _():
        eq[:, 0:QKV_HIST, :] = jnp.zeros((nb, QKV_HIST, width), F32)
        s_scr[...] = jnp.zeros(s_scr.shape, F32)

    @pl.when(t > 0)
    def _():
        eq[:, 0:QKV_HIST, :] = eq[:, tt:tt + QKV_HIST, :]

    eq[:, QKV_HIST:, :] = qkv_ref[...]

    for r in range(1, DN_CONV):
        eqs[:, r - 1] = eq[:, QKV_HIST - r:QKV_HIST - r + tt, :]

    def conv_cols(bi, lo):
        acc = cw_ref[DN_CONV - 1:DN_CONV, lo:lo + LANES] * eq[bi, QKV_HIST:QKV_HIST + tt, lo:lo + LANES]
        for r in range(1, DN_CONV):
            acc = acc + cw_ref[DN_CONV - 1 - r:DN_CONV - r, lo:lo + LANES] * eqs[bi, r - 1, :, lo:lo + LANES]
        return _silu(acc)

    def split3(x):
        hi = x.astype(BF16)
        r = x - hi.astype(F32)
        mid = r.astype(BF16)
        return hi, mid, (r - mid.astype(F32)).astype(BF16)

    def dot3(a, pieces, left):
        return sum(jnp.dot(a, p, preferred_element_type=F32) if left else
                   jnp.dot(p, a, preferred_element_type=F32) for p in pieces)

    def l2n(x):
        return x * lax.rsqrt(_lane_sum(x * x) + EPS)

    rc = lax.broadcasted_iota(jnp.int32, (chunk, chunk), 0)
    cc = lax.broadcasted_iota(jnp.int32, (chunk, chunk), 1)
    incl = rc >= cc
    strict = rc > cc
    masks = _tri_masks(chunk)
    chunks = range(tt // chunk)
    rows = [slice(c * chunk, (c + 1) * chunk) for c in chunks]

    seqs = [(bi, h) for bi in range(nb) for h in range(DN_HEADS)]
    q, k, v, beta, gcol, grow, eg = [], [], [], [], [], [], []
    for bi in range(nb):
        bg3 = split3(bg_ref[bi])
        gc = dot3(ltri_ref[...], bg3, True)
        gt = gc.T
        beta_d = dot3(selb_ref[...], bg3, False)
        g_d = dot3(selg_ref[...], split3(gc), False)
        eg_d = jnp.exp(g_d)
        for h in range(DN_HEADS):
            cols = slice(h * dk, (h + 1) * dk)
            q.append(l2n(conv_cols(bi, h * dk)) * (dk ** -0.5))
            k.append(l2n(conv_cols(bi, DN_WIDTH + h * dk)))
            v.append(conv_cols(bi, 2 * DN_WIDTH + h * dk))
            beta.append(beta_d[:, cols])
            gcol.append(g_d[:, cols])
            eg.append(eg_d[:, cols])
            grow.append(gt[DN_HEADS + h:DN_HEADS + h + 1, :])
    ns = range(len(seqs))
    qg = [q[s] * eg[s] for s in ns]
    rhs_all = [jnp.concatenate([v[s] * beta[s], k[s] * (beta[s] * eg[s])], axis=1) for s in ns]

    pairs = [(s, c) for s in ns for c in chunks]
    kq = [_dot(jnp.concatenate([k[s][rows[c]], q[s][rows[c]]], axis=0), k[s][rows[c]].T)
          for s, c in pairs]
    decay = [jnp.where(incl, jnp.exp(jnp.where(incl, gcol[s][rows[c]] - grow[s][:, rows[c]], 0.0)), 0.0)
             for s, c in pairs]
    bmat = [jnp.where(strict, -(beta[s][rows[c]] * kq[i][:chunk] * decay[i]), 0.0)
            for i, (s, c) in enumerate(pairs)]
    qk = [kq[i][chunk:] * decay[i] for i in range(len(pairs))]
    sol = _tri_solve(bmat, [rhs_all[s][rows[c]] for s, c in pairs], chunk, masks)
    g_last = [gcol[s][(c + 1) * chunk - 1:(c + 1) * chunk] for s, c in pairs]
    k_tail = [k[s][rows[c]] * jnp.exp(g_last[i] - gcol[s][rows[c]]) for i, (s, c) in enumerate(pairs)]

    state = [s_scr[bi, h] for bi, h in seqs]
    o = [[] for _ in ns]
    for c in chunks:
        idx = [pairs.index((s, c)) for s in ns]
        ws = [_dot(jnp.concatenate([sol[idx[s]][:, dk:], qg[s][rows[c]]], axis=0), state[s]) for s in ns]
        u = [sol[idx[s]][:, :dk] - ws[s][:chunk] for s in ns]
        for s in ns:
            o[s].append(ws[s][chunk:] + _dot(qk[idx[s]], u[s]))
        state = [state[s] * jnp.exp(g_last[idx[s]]) + _dot(k_tail[idx[s]].T, u[s]) for s in ns]
    for s, (bi, h) in enumerate(seqs):
        s_scr[bi, h] = state[s]
        os_ = jnp.concatenate(o[s], axis=0)
        rms = lax.rsqrt(_lane_sum(os_ * os_) * (1.0 / dk) + EPS)
        y_o[bi, :, h * dk:(h + 1) * dk] = os_ * rms * ng_ref[...] * _silu(z_ref[bi, :, h * dk:(h + 1) * dk])

    @pl.when(t == pl.num_programs(1) - 1)
    def _():
        s_o[...] = s_scr[...]


def _gdn(qkv, z, bg, layer, batch, seq, cw, ng, consts, nb, tt):
    assert GDN_CHUNK == LANES and DN_HEAD_DIM == LANES
    blk = lambda w: pl.BlockSpec((nb, tt, w), lambda p, t: (p, t, 0))
    const3 = lambda a, b_: pl.BlockSpec((None, a, b_), lambda p, t: (layer, 0, 0))
    const2 = lambda a, b_: pl.BlockSpec((a, b_), lambda p, t: (0, 0))
    state = (DN_HEADS, DN_HEAD_DIM, DN_HEAD_DIM)
    seq3 = lambda a: a.reshape(batch, seq, a.shape[-1])
    y, s_new = pl.pallas_call(
        functools.partial(_gdn_kernel, nb=nb, tt=tt, chunk=GDN_CHUNK),
        grid=(batch // nb, seq // tt),
        in_specs=[blk(3 * DN_WIDTH), blk(DN_WIDTH), blk(LANES),
                  const3(DN_CONV, 3 * DN_WIDTH), const3(1, DN_HEAD_DIM),
                  const2(tt, tt), const2(LANES, DN_WIDTH), const2(LANES, DN_WIDTH)],
        out_specs=[blk(DN_WIDTH),
                   pl.BlockSpec((nb,) + state, lambda p, t: (p, 0, 0, 0))],
        out_shape=[jax.ShapeDtypeStruct((batch, seq, DN_WIDTH), F32),
                   jax.ShapeDtypeStruct((batch,) + state, F32)],
        scratch_shapes=[pltpu.VMEM((nb, QKV_HIST + tt, 3 * DN_WIDTH), F32),
                        pltpu.VMEM((nb, DN_CONV - 1, tt, 3 * DN_WIDTH), F32),
                        pltpu.VMEM((nb,) + state, F32)],
        compiler_params=_cparams(("parallel", "arbitrary")),
        name="gated_delta",
    )(seq3(qkv), seq3(z), seq3(bg), cw, ng, *consts)
    return y.reshape(batch * seq, DN_WIDTH), s_new


def _gdn_step_kernel(qkv_ref, sq_ref, z_ref, bg_ref, s_ref, cw_ref, ng_ref, *rest):
    y_o, nq_o, s_o = rest[-3:]
    width = 3 * DN_WIDTH
    qkv = qkv_ref[...]
    bt = qkv.shape[0]
    acc = cw_ref[DN_CONV - 1:DN_CONV, :] * qkv
    for j in range(DN_CONV - 1):
        acc = acc + cw_ref[j:j + 1, :] * sq_ref[:, j, :]
    act = _silu(acc)
    nq_o[:, 0:DN_CONV - 2, :] = sq_ref[:, 1:DN_CONV - 1, :]
    nq_o[:, DN_CONV - 2, :] = qkv
    bg = bg_ref[...]
    for h in range(DN_HEADS):
        cols = slice(h * DN_HEAD_DIM, (h + 1) * DN_HEAD_DIM)
        q = _l2norm(act[:, cols]) * (DN_HEAD_DIM ** -0.5)
        k = _l2norm(act[:, DN_WIDTH + h * DN_HEAD_DIM:DN_WIDTH + (h + 1) * DN_HEAD_DIM])
        v = act[:, 2 * DN_WIDTH + h * DN_HEAD_DIM:2 * DN_WIDTH + (h + 1) * DN_HEAD_DIM]
        beta = bg[:, h:h + 1]
        eg = jnp.exp(bg[:, DN_HEADS + h:DN_HEADS + h + 1])
        qk = jnp.sum(q * k, axis=-1, keepdims=True)
        outs = []
        for b in range(bt):
            s = s_ref[b, h]
            k_col = jnp.broadcast_to(k[b:b + 1], (DN_HEAD_DIM, DN_HEAD_DIM)).T
            q_col = jnp.broadcast_to(q[b:b + 1], (DN_HEAD_DIM, DN_HEAD_DIM)).T
            ks = jnp.sum(k_col * s, axis=0, keepdims=True)
            qs = jnp.sum(q_col * s, axis=0, keepdims=True)
            e = eg[b:b + 1]
            u = beta[b:b + 1] * (v[b:b + 1] - e * ks)
            outs.append(e * qs + qk[b:b + 1] * u)
            s_o[b, h] = e * s + k_col * u
        o = jnp.concatenate(outs, axis=0)
        y_o[:, cols] = _gated_norm(o, ng_ref[...], z_ref[:, cols])


def _carried(prev, first_input, first_output):
    specs = [pl.BlockSpec(memory_space=pl.ANY) for _ in prev]
    return specs, {first_input + j: first_output + j for j in range(len(prev))}


def _gdn_step(qkv, sq_all, z, bg, s_all, prev, layer, cw, ng, bt):
    n = qkv.shape[0]
    row = lambda w: pl.BlockSpec((bt, w), lambda i: (i, 0))
    const3 = lambda a, b_: pl.BlockSpec((None, a, b_), lambda i: (layer, 0, 0))
    sblk = pl.BlockSpec((None, bt, DN_HEADS, DN_HEAD_DIM, DN_HEAD_DIM), lambda i: (layer, i, 0, 0, 0))
    qblk = pl.BlockSpec((None, bt) + sq_all.shape[2:], lambda i: (layer, i, 0, 0))
    carry_specs, aliases = _carried(prev, 7, 1)
    return pl.pallas_call(
        _gdn_step_kernel,
        grid=(n // bt,),
        in_specs=[row(3 * DN_WIDTH), qblk, row(DN_WIDTH), row(LANES), sblk,
                  const3(DN_CONV, 3 * DN_WIDTH), const3(1, DN_HEAD_DIM)] + carry_specs,
        out_specs=[row(DN_WIDTH), qblk, sblk],
        out_shape=[jax.ShapeDtypeStruct(z.shape, F32), jax.ShapeDtypeStruct(sq_all.shape, F32),
                   jax.ShapeDtypeStruct(s_all.shape, F32)],
        input_output_aliases=aliases,
        compiler_params=_cparams(("parallel",)),
        name="gated_delta_step",
    )(qkv, sq_all, z, bg, s_all, cw, ng, *prev)


def _mix_ffn_kernel(x_ref, ya_ref, yb_ref, yc_ref, gate1_ref, shift_ref, scale_ref, gate_ref, g_ref,
                    wo_ref, w1_ref, w2_ref, gf_ref, o_ref, *, final, hidden_chunk):
    a_end, b_end = POOL_WIDTH, POOL_WIDTH + DN_WIDTH
    mix = (_dot(ya_ref[...], wo_ref[0:a_end, :]) + _dot(yb_ref[...], wo_ref[a_end:b_end, :])
           + _dot(yc_ref[...], wo_ref[b_end:, :]))
    x = x_ref[...] + gate1_ref[...] * mix
    h = _mod_rmsnorm(x, g_ref[...], scale_ref[...], shift_ref[...]).astype(BF16)
    acc = jnp.zeros(x.shape, F32)
    for c in range(0, D_FF, hidden_chunk):
        a = jnp.maximum(jnp.dot(h, w1_ref[:, c:c + hidden_chunk], preferred_element_type=F32), 0.0)
        acc = acc + jnp.dot((a * a).astype(BF16), w2_ref[c:c + hidden_chunk, :],
                            preferred_element_type=F32)
    y = x + gate_ref[...] * acc
    if final:
        y = y * lax.rsqrt(jnp.mean(y * y, axis=-1, keepdims=True) + EPS) * gf_ref[...]
    o_ref[...] = y


def _mix_ffn(x, ya, yb, yc, mod, mode, layer, rows_per_batch, g2, w_out, w1, w2, g_final, final, tm):
    n = x.shape[0]
    row = lambda w: pl.BlockSpec((tm, w), lambda i: (i, 0))
    resident = lambda a, b: pl.BlockSpec((None, a, b), lambda i: (layer, 0, 0),
                                         pipeline_mode=pl.Buffered(1))
    return pl.pallas_call(
        functools.partial(_mix_ffn_kernel, final=final, hidden_chunk=1024),
        grid=(n // tm,),
        in_specs=[row(D_MODEL), row(POOL_WIDTH), row(DN_WIDTH), row(CONF_WIDTH),
                  _mod_spec(mode, 2, tm, rows_per_batch),
                  _mod_spec(mode, 3, tm, rows_per_batch),
                  _mod_spec(mode, 4, tm, rows_per_batch),
                  _mod_spec(mode, 5, tm, rows_per_batch),
                  pl.BlockSpec((None, 1, D_MODEL), lambda i: (layer, 0, 0)),
                  resident(D_MODEL, D_MODEL), resident(D_MODEL, D_FF), resident(D_FF, D_MODEL),
                  pl.BlockSpec((1, D_MODEL), lambda i: (0, 0))],
        out_specs=row(D_MODEL),
        out_shape=jax.ShapeDtypeStruct(x.shape, F32),
        compiler_params=_cparams(("parallel",)),
        name="mix_ffn",
    )(x, ya, yb, yc, mod, mod, mod, mod, g2, w_out, w1, w2, g_final)


def kernel(x_prompt, x_sample, state_pool, state_qkv_conv, state_delta, state_conv, c_prompt, c_sample,
           w_ada, b_ada, g_norm1, g_norm2, w_in, pool_w, pool_scale, qkv_conv_w, a_log, dt_bias,
           dn_norm_g, conf_dw_w, conf_dw_b, conf_ln_g, conf_ln_b, conf_pw_w, w_out, w_ff1, w_ff2, g_final):
    depth = w_in.shape[0]
    bp, seq, d = x_prompt.shape
    bs = x_sample.shape[0]
    assert d == D_MODEL and x_sample.shape[1] == 1

    w_in_r = w_in.astype(BF16)
    groups = pool_w.shape[1]
    pw_bd = jnp.einsum("lgcd,gh->lgchd", pool_w, jnp.eye(groups, dtype=pool_w.dtype))
    pw_bd = pw_bd.reshape(depth, POOL_WIDTH, POOL_WIDTH).astype(BF16)
    head_of = jnp.arange(CONF_WIDTH) // (CONF_WIDTH // CONF_HEADS)
    avg = jnp.where(head_of[:, None] == head_of[None, :], CONF_HEADS / CONF_WIDTH, 0.0).astype(BF16)
    tpos = jnp.arange(GDN_TILE)
    ltri = ((tpos[:, None] >= tpos[None, :])
            & (tpos[:, None] // GDN_CHUNK == tpos[None, :] // GDN_CHUNK)).astype(BF16)
    lane_id = jnp.arange(LANES)[:, None]
    head_id = jnp.arange(DN_WIDTH)[None, :] // DN_HEAD_DIM
    gdn_consts = (ltri, (lane_id == head_id).astype(BF16), (lane_id == DN_HEADS + head_id).astype(BF16))
    lane_pad = lambda a: jnp.pad(a, ((0, 0), (DN_HEADS, LANES - 2 * DN_HEADS)))[:, None, :]
    alog_row, dtb_row = lane_pad(a_log), lane_pad(dt_bias)
    r3 = lambda a: a[:, None, :]
    g1, g2, pscale, ng = r3(g_norm1), r3(g_norm2), r3(pool_scale), r3(dn_norm_g)
    cb, lng, lnb = r3(conf_dw_b), r3(conf_ln_g), r3(conf_ln_b)
    cpw = conf_pw_w.astype(BF16)
    w_out_b, w1_b, w2_b = w_out.astype(BF16), w_ff1.astype(BF16), w_ff2.astype(BF16)
    gf = g_final[None, :]

    mods = _ada(jnp.concatenate([c_prompt, c_sample], axis=0), w_ada, b_ada)
    mods_p = mods[:, :bp].reshape(depth, bp * 6, 1, d)
    mods_s = mods[:, bp:]

    xp = x_prompt.reshape(bp * seq, d)
    xs = x_sample.reshape(bs, d)
    tm_p, nb_gdn, tt_gdn, bt_mix, bt_gdn = 512, GDN_SEQS, GDN_TILE, min(32, bs), min(16, bs)
    pc_weights = (pw_bd, pscale, conf_dw_w, cb, lng, lnb, cpw, avg)
    outs = {0: [], 2: [], 4: [], 6: []}
    sp_all, sc_all, sq_all = state_pool, state_conv, state_qkv_conv
    pc_carry = (jnp.zeros_like(sp_all), jnp.zeros_like(sc_all))
    gdn_carry = (jnp.zeros_like(sq_all), jnp.zeros_like(state_delta))
    for l in range(depth):
        last = l == depth - 1
        mp = mods_p[l]
        u, qkv, z, glu, bg, ya, yc = _inproj(xp, mp, "batch", l, seq, g1, w_in_r, alog_row, dtb_row, tm_p,
                                             poolconf_weights=pc_weights)
        yb, s_new = _gdn(qkv, z, bg, l, bp, seq, qkv_conv_w, ng, gdn_consts, nb_gdn, tt_gdn)
        xp = _mix_ffn(xp, ya, yb, yc, mp, "batch", l, seq, g2, w_out_b, w1_b, w2_b, gf, last, tm_p)
        outs[0].append(u.reshape(bp, seq, -1)[:, seq - POOL_BUF:])
        outs[2].append(qkv.reshape(bp, seq, -1)[:, seq - (DN_CONV - 1):])
        outs[4].append(s_new)
        outs[6].append(glu.reshape(bp, seq, -1)[:, seq - (CONF_K - 1):])
        ms = mods_s[l]
        u, qkv, z, glu, bg = _inproj(xs, ms, "row", l, 1, g1, w_in_r, alog_row, dtb_row, bs)
        ya, yc, *pc_carry = _poolconf_step(u, sp_all, glu, sc_all, pc_carry, l, *pc_weights, bt_mix)
        yb, *gdn_carry = _gdn_step(qkv, sq_all, z, bg, state_delta, gdn_carry, l, qkv_conv_w, ng, bt_gdn)
        xs = _mix_ffn(xs, ya, yb, yc, ms, "row", l, 1, g2, w_out_b, w1_b, w2_b, gf, last, bs)
    n_pool, n_conf = pc_carry
    n_qkv, n_delta = gdn_carry
    return (xp.reshape(bp, seq, d), xs.reshape(bs, 1, d),
            jnp.stack(outs[0]), n_pool, jnp.stack(outs[2]), n_qkv,
            jnp.stack(outs[4]), n_delta, jnp.stack(outs[6]), n_conf)
```

```python
import functools

import jax
import jax.numpy as jnp
from jax import lax
from jax.experimental import pallas as pl
from jax.experimental.pallas import tpu as pltpu

F32 = jnp.float32
BF16 = jnp.bfloat16
EPS = 1e-6

D_MODEL = 1024
POOL_WIDTH = 256
POOL_WINDOWS = (2, 4, 8, 16)
POOL_BUF = 15
DN_WIDTH = 512
DN_HEAD_DIM = 128
DN_HEADS = 4
DN_CONV = 4
CONF_WIDTH = 256
CONF_HEADS = 4
CONF_K = 31
D_FF = 4 * D_MODEL
PAST_LEN = 16384
OFF_QKV = POOL_WIDTH
OFF_Z = OFF_QKV + 3 * DN_WIDTH
OFF_B = OFF_Z + DN_WIDTH
OFF_A = OFF_B + DN_HEADS
OFF_GLU = OFF_A + DN_HEADS
N_IN = OFF_GLU + 2 * CONF_WIDTH

LANES = 128
POOL_HIST = 32
CONF_HIST = 32
QKV_HIST = 8
GDN_CHUNK = 128
GDN_TILE = 128
GDN_SEQS = 4
SOLVE_BASE = 16
VMEM_LIMIT = 56 * 1024 * 1024


def _cparams(sem):
    return pltpu.CompilerParams(dimension_semantics=sem, vmem_limit_bytes=VMEM_LIMIT)


def _dot(a, b):
    return jnp.dot(a.astype(BF16), b.astype(BF16), preferred_element_type=F32)


def _lane_sum(x):
    ones = jnp.ones((x.shape[1], LANES), BF16)
    return jnp.dot(x.astype(BF16), ones, preferred_element_type=F32)


def _split_dot(x, m):
    hi = x.astype(BF16)
    lo = (x - hi.astype(F32)).astype(BF16)
    return (jnp.dot(hi, m, preferred_element_type=F32)
            + jnp.dot(lo, m, preferred_element_type=F32))


def _sigmoid(x):
    return 1.0 / (1.0 + jnp.exp(-x))


def _silu(x):
    return x * _sigmoid(x)


def _softplus(x):
    return jnp.maximum(x, 0.0) + jnp.log1p(jnp.exp(-jnp.abs(x)))


def _mod_rmsnorm(x, g, scale, shift):
    y = x * lax.rsqrt(jnp.mean(x * x, axis=-1, keepdims=True) + EPS)
    return (y * g) * (1.0 + scale) + shift


def _ada_kernel(c_ref, w_ref, b_ref, o_ref):
    o_ref[...] = _dot(_silu(c_ref[...]), w_ref[...]) + b_ref[...]


def _ada(c_all, w_ada, b_ada):
    depth, d, n = w_ada.shape
    rows = c_all.shape[0]
    tn = 1536
    return pl.pallas_call(
        _ada_kernel,
        grid=(depth, n // tn),
        in_specs=[pl.BlockSpec((rows, d), lambda l, j: (0, 0)),
                  pl.BlockSpec((None, d, tn), lambda l, j: (l, 0, j)),
                  pl.BlockSpec((None, 1, tn), lambda l, j: (l, 0, j))],
        out_specs=pl.BlockSpec((None, rows, tn), lambda l, j: (l, 0, j)),
        out_shape=jax.ShapeDtypeStruct((depth, rows, n), F32),
        compiler_params=_cparams(("arbitrary", "arbitrary")),
        name="ada_mod",
    )(c_all, w_ada, b_ada.reshape(depth, 1, n))


N_POOLCONF_REFS = 8


def _inproj_kernel(x_ref, shift_ref, scale_ref, g_ref, w_ref, alog_ref, dtb_ref, *rest, tm, tiles_per_seq):
    fused = tiles_per_seq is not None
    pc_refs, rest = (rest[:N_POOLCONF_REFS], rest[N_POOLCONF_REFS:]) if fused else ((), rest)
    pool_o, qkv_o, z_o, glu_o, bg_o = rest[:5]
    h = _mod_rmsnorm(x_ref[...], g_ref[...], scale_ref[...], shift_ref[...]).astype(BF16)

    def seg(lo, hi):
        return jnp.dot(h, w_ref[:, lo:hi], preferred_element_type=F32)

    pool_o[...] = seg(0, OFF_QKV)
    tail = seg(OFF_B, N_IN)
    glu_lo = OFF_GLU - OFF_B
    glu_o[...] = (tail[:, glu_lo:glu_lo + CONF_WIDTH]
                  * _sigmoid(tail[:, glu_lo + CONF_WIDTH:glu_lo + 2 * CONF_WIDTH]))
    ba = tail[:, :LANES]
    lane = lax.broadcasted_iota(jnp.int32, ba.shape, 1)
    beta = _sigmoid(ba)
    g = -jnp.exp(alog_ref[...]) * _softplus(ba + dtb_ref[...])
    bg_o[...] = jnp.where(lane < DN_HEADS, beta, g)
    if fused:
        ya_o, yc_o = rest[5:7]
        t = pl.program_id(0) % tiles_per_seq
        ya_o[...], yc_o[...] = _poolconf_tile(t, pool_o, glu_o, *pc_refs, *rest[7:], tt=tm, sub=64)
    qkv_o[...] = seg(OFF_QKV, OFF_Z)
    z_o[...] = seg(OFF_Z, OFF_B)


def _mod_spec(mode, chunk, tm, rows_per_batch):
    if mode == "batch":
        per = rows_per_batch // tm
        return pl.BlockSpec((None, 1, D_MODEL), lambda i: ((i // per) * 6 + chunk, 0, 0))
    return pl.BlockSpec((tm, D_MODEL), lambda i: (i, chunk))


def _inproj(x, mod, mode, layer, rows_per_batch, g1, w_in_r, alog_row, dtb_row, tm, poolconf_weights=None):
    n = x.shape[0]
    row = lambda w: pl.BlockSpec((tm, w), lambda i: (i, 0))
    const3 = lambda a, b: pl.BlockSpec((None, a, b), lambda i: (layer, 0, 0))
    outs = [POOL_WIDTH, 3 * DN_WIDTH, DN_WIDTH, CONF_WIDTH, LANES]
    in_specs = [row(D_MODEL),
                _mod_spec(mode, 0, tm, rows_per_batch),
                _mod_spec(mode, 1, tm, rows_per_batch),
                const3(1, D_MODEL),
                pl.BlockSpec((None, D_MODEL, N_IN), lambda i: (layer, 0, 0), pipeline_mode=pl.Buffered(1)),
                const3(1, LANES), const3(1, LANES)]
    args = (x, mod, mod, g1, w_in_r, alog_row, dtb_row)
    if poolconf_weights is None:
        tiles_per_seq, scratch, sem = None, [], "parallel"
    else:
        tiles_per_seq, scratch, sem = rows_per_batch // tm, _poolconf_scratch(tm), "arbitrary"
        in_specs += [const3(POOL_WIDTH, POOL_WIDTH), const3(1, POOL_WIDTH),
                     const3(CONF_K, CONF_WIDTH), const3(1, CONF_WIDTH),
                     const3(1, CONF_WIDTH), const3(1, CONF_WIDTH), const3(CONF_WIDTH, CONF_WIDTH),
                     pl.BlockSpec((CONF_WIDTH, CONF_WIDTH), lambda i: (0, 0))]
        args += tuple(poolconf_weights)
        outs += [POOL_WIDTH, CONF_WIDTH]
    return pl.pallas_call(
        functools.partial(_inproj_kernel, tm=tm, tiles_per_seq=tiles_per_seq),
        grid=(n // tm,),
        in_specs=in_specs,
        out_specs=[row(w) for w in outs],
        out_shape=[jax.ShapeDtypeStruct((n, w), F32) for w in outs],
        scratch_shapes=scratch,
        compiler_params=_cparams((sem,)),
        name="in_proj",
    )(*args)


def _pool_delta(win, u, pos, half):
    w_lo, w_hi = POOL_WINDOWS[2 * half], POOL_WINDOWS[2 * half + 1]
    a = win(0)
    for j in range(1, w_lo):
        a = a + win(j)
    b = win(w_lo)
    for j in range(w_lo + 1, w_hi):
        b = b + win(j)
    lane = lax.broadcasted_iota(jnp.int32, u.shape, 1)
    upper = lane >= LANES // 2
    s = a + jnp.where(upper, b, 0.0)
    cnt = jnp.where(upper, jnp.minimum(pos + 1, w_hi), jnp.minimum(pos + 1, w_lo)).astype(F32)
    return s / cnt - u


def _conf_post(dc, avg, lng, lnb, cpw):
    mu = _split_dot(dc, avg)
    xc = dc - mu
    var = _split_dot(xc * xc, avg)
    dn = xc * lax.rsqrt(var + EPS) * lng + lnb
    return _dot(_silu(dn), cpw)


def _poolconf_tile(t, u_ref, glu_ref, pw_ref, ps_ref, cw_ref, cb_ref, lng_ref, lnb_ref, cpw_ref,
                   avg_ref, epool, econf, pa, pb, erot, *, tt, sub):
    n = POOL_HIST + tt

    @pl.when(t == 0)
    def _():
        epool[0:POOL_HIST, :] = jnp.zeros((POOL_HIST, POOL_WIDTH), F32)
        econf[0:CONF_HIST, :] = jnp.zeros((CONF_HIST, CONF_WIDTH), F32)

    @pl.when(t > 0)
    def _():
        epool[0:POOL_HIST, :] = epool[tt:tt + POOL_HIST, :]
        econf[0:CONF_HIST, :] = econf[tt:tt + CONF_HIST, :]

    epool[POOL_HIST:, :] = u_ref[...]
    econf[CONF_HIST:, :] = glu_ref[...]

    pos = t * tt + lax.broadcasted_iota(jnp.int32, (tt, 1), 0)
    upper = lax.broadcasted_iota(jnp.int32, (tt, LANES), 1) >= LANES // 2
    halves = []
    for half in range(2):
        cols = slice(half * LANES, (half + 1) * LANES)
        w_lo, w_hi = POOL_WINDOWS[2 * half], POOL_WINDOWS[2 * half + 1]
        pa[8:n, :] = epool[8:n, cols] + epool[7:n - 1, cols]
        pb[16:n, :] = pa[16:n, :] + pa[14:n - 2, :]
        if half == 0:
            s_lo, s_hi = pa[POOL_HIST:n, :], pb[POOL_HIST:n, :]
        else:
            pa[24:n, :] = pb[24:n, :] + pb[20:n - 4, :]
            s_lo = pa[POOL_HIST:n, :]
            s_hi = s_lo + pa[POOL_HIST - 8:n - 8, :]
        cnt = jnp.where(upper, jnp.minimum(pos + 1, w_hi), jnp.minimum(pos + 1, w_lo)).astype(F32)
        halves.append(jnp.where(upper, s_hi, s_lo) / cnt - epool[POOL_HIST:n, cols])
    d = jnp.concatenate(halves, axis=1)
    ya = _dot(d, pw_ref[...]) * ps_ref[...]

    m = CONF_HIST + tt - 8
    for r in range(1, 8):
        erot[r - 1, 0:m, :] = econf[r:r + m, :]
    first = CONF_HIST - (CONF_K - 1)
    dc_blocks = []
    for r0 in range(0, tt, sub):
        acc = cb_ref[...]
        for j in range(CONF_K):
            a8, r = divmod(first + j, 8)
            lo = 8 * a8 + r0
            rows = econf[lo:lo + sub, :] if r == 0 else erot[r - 1, lo:lo + sub, :]
            acc = acc + cw_ref[j:j + 1, :] * rows
        dc_blocks.append(acc)
    dc = jnp.concatenate(dc_blocks, axis=0)
    return ya, _conf_post(dc, avg_ref[...], lng_ref[...], lnb_ref[...], cpw_ref[...])


def _poolconf_scratch(tt):
    return [pltpu.VMEM((POOL_HIST + tt, POOL_WIDTH), F32),
            pltpu.VMEM((CONF_HIST + tt, CONF_WIDTH), F32),
            pltpu.VMEM((POOL_HIST + tt, LANES), F32),
            pltpu.VMEM((POOL_HIST + tt, LANES), F32),
            pltpu.VMEM((7, CONF_HIST + tt, CONF_WIDTH), F32)]


def _poolconf_step_kernel(u_ref, sp_ref, glu_ref, sc_ref, pw_ref, ps_ref, cw_ref, cb_ref, lng_ref,
                          lnb_ref, cpw_ref, avg_ref, *rest, pos0):
    ya_o, yc_o, np_o, nc_o = rest[-4:]
    u = u_ref[...]
    rows = u.shape[0]
    pos = jnp.full((rows, 1), pos0, jnp.int32)
    halves = []
    for half in range(2):
        def win(j, half=half):
            if j == 0:
                return u[:, half * LANES:(half + 1) * LANES]
            return sp_ref[:, POOL_BUF - j, half * LANES:(half + 1) * LANES]
        halves.append(_pool_delta(win, win(0), pos, half))
    d = jnp.concatenate(halves, axis=1)
    ya_o[...] = _dot(d, pw_ref[...]) * ps_ref[...]
    np_o[:, 0:POOL_BUF - 1, :] = sp_ref[:, 1:POOL_BUF, :]
    np_o[:, POOL_BUF - 1, :] = u

    glu = glu_ref[...]
    acc = cb_ref[...] + cw_ref[CONF_K - 1:CONF_K, :] * glu
    for j in range(CONF_K - 1):
        acc = acc + cw_ref[j:j + 1, :] * sc_ref[:, j, :]
    yc_o[...] = _conf_post(acc, avg_ref[...], lng_ref[...], lnb_ref[...], cpw_ref[...])
    nc_o[:, 0:CONF_K - 2, :] = sc_ref[:, 1:CONF_K - 1, :]
    nc_o[:, CONF_K - 2, :] = glu


def _poolconf_step(u, sp_all, glu, sc_all, prev, layer, pw_bd, pscale, cw, cb, lng, lnb, cpw, avg, bt):
    n = u.shape[0]
    row = lambda w: pl.BlockSpec((bt, w), lambda i: (i, 0))
    const3 = lambda a, b_: pl.BlockSpec((None, a, b_), lambda i: (layer, 0, 0))
    state = lambda a: pl.BlockSpec((None, bt) + a.shape[2:], lambda i: (layer, i, 0, 0))
    carry_specs, aliases = _carried(prev, 12, 2)
    return pl.pallas_call(
        functools.partial(_poolconf_step_kernel, pos0=PAST_LEN),
        grid=(n // bt,),
        in_specs=[row(POOL_WIDTH), state(sp_all), row(CONF_WIDTH), state(sc_all),
                  const3(POOL_WIDTH, POOL_WIDTH), const3(1, POOL_WIDTH),
                  const3(CONF_K, CONF_WIDTH), const3(1, CONF_WIDTH),
                  const3(1, CONF_WIDTH), const3(1, CONF_WIDTH),
                  const3(CONF_WIDTH, CONF_WIDTH),
                  pl.BlockSpec((CONF_WIDTH, CONF_WIDTH), lambda i: (0, 0))] + carry_specs,
        out_specs=[row(POOL_WIDTH), row(CONF_WIDTH), state(sp_all), state(sc_all)],
        out_shape=[jax.ShapeDtypeStruct((n, POOL_WIDTH), F32), jax.ShapeDtypeStruct((n, CONF_WIDTH), F32),
                   jax.ShapeDtypeStruct(sp_all.shape, F32), jax.ShapeDtypeStruct(sc_all.shape, F32)],
        input_output_aliases=aliases,
        compiler_params=_cparams(("parallel",)),
        name="pool_conf_step",
    )(u, sp_all, glu, sc_all, pw_bd, pscale, cw, cb, lng, lnb, cpw, avg, *prev)


def _l2norm(x):
    return x * lax.rsqrt(jnp.sum(x * x, axis=-1, keepdims=True) + EPS)


def _gated_norm(o, ng, z):
    return o * lax.rsqrt(jnp.mean(o * o, axis=-1, keepdims=True) + EPS) * ng * _silu(z)


def _tri_masks(c):
    ri = lax.broadcasted_iota(jnp.int32, (c, c), 0)
    ci = lax.broadcasted_iota(jnp.int32, (c, c), 1)
    masks = [ri // SOLVE_BASE == ci // SOLVE_BASE]
    size = SOLVE_BASE
    while size < c:
        masks.append((ri // (2 * size) == ci // (2 * size)) & (ri // size != ci // size))
        size *= 2
    return masks


def _tri_solve(bs, rhss, c, masks):
    ns = [jnp.where(masks[0], b, 0.0) for b in bs]
    ps = [_dot(n, n) for n in ns]
    power = 2
    while power < SOLVE_BASE:
        if 2 * power < SOLVE_BASE:
            sts = [_dot(jnp.concatenate([n, p], axis=0), p) for n, p in zip(ns, ps)]
            ns = [n + p + st[:c] for n, p, st in zip(ns, ps, sts)]
            ps = [st[c:] for st in sts]
        else:
            ns = [n + p + _dot(n, p) for n, p in zip(ns, ps)]
        power *= 2
    for m in masks[1:]:
        ls = [jnp.where(m, b, 0.0) for b in bs]
        tls = [l + _dot(n, l) for n, l in zip(ns, ls)]
        ns = [n + tl + _dot(tl, n) for n, tl in zip(ns, tls)]
    return [rhs + _dot(n, rhs) for n, rhs in zip(ns, rhss)]


def _gdn_kernel(qkv_ref, z_ref, bg_ref, cw_ref, ng_ref, ltri_ref, selb_ref, selg_ref, y_o, s_o,
                eq, eqs, s_scr, *, nb, tt, chunk):
    t = pl.program_id(1)
    width = 3 * DN_WIDTH
    dk = DN_HEAD_DIM

    @pl.when(t == 0)
    def _():
        eq[:, 0:QKV_HIST, :] = jnp.zeros((nb, QKV_HIST, width), F32)
        s_scr[...] = jnp.zeros(s_scr.shape, F32)

    @pl.when(t > 0)
    def _():
        eq[:, 0:QKV_HIST, :] = eq[:, tt:tt + QKV_HIST, :]

    eq[:, QKV_HIST:, :] = qkv_ref[...]

    for r in range(1, DN_CONV):
        eqs[:, r - 1] = eq[:, QKV_HIST - r:QKV_HIST - r + tt, :]

    def conv_cols(bi, lo):
        acc = cw_ref[DN_CONV - 1:DN_CONV, lo:lo + LANES] * eq[bi, QKV_HIST:QKV_HIST + tt, lo:lo + LANES]
        for r in range(1, DN_CONV):
            acc = acc + cw_ref[DN_CONV - 1 - r:DN_CONV - r, lo:lo + LANES] * eqs[bi, r - 1, :, lo:lo + LANES]
        return _silu(acc)

    def split3(x):
        hi = x.astype(BF16)
        r = x - hi.astype(F32)
        mid = r.astype(BF16)
        return hi, mid, (r - mid.astype(F32)).astype(BF16)

    def dot3(a, pieces, left):
        return sum(jnp.dot(a, p, preferred_element_type=F32) if left else
                   jnp.dot(p, a, preferred_element_type=F32) for p in pieces)

    def l2n(x):
        return x * lax.rsqrt(_lane_sum(x * x) + EPS)

    rc = lax.broadcasted_iota(jnp.int32, (chunk, chunk), 0)
    cc = lax.broadcasted_iota(jnp.int32, (chunk, chunk), 1)
    incl = rc >= cc
    strict = rc > cc
    masks = _tri_masks(chunk)
    chunks = range(tt // chunk)
    rows = [slice(c * chunk, (c + 1) * chunk) for c in chunks]

    seqs = [(bi, h) for bi in range(nb) for h in range(DN_HEADS)]
    q, k, v, beta, gcol, grow, eg = [], [], [], [], [], [], []
    for bi in range(nb):
        bg3 = split3(bg_ref[bi])
        gc = dot3(ltri_ref[...], bg3, True)
        gt = gc.T
        beta_d = dot3(selb_ref[...], bg3, False)
        g_d = dot3(selg_ref[...], split3(gc), False)
        eg_d = jnp.exp(g_d)
        for h in range(DN_HEADS):
            cols = slice(h * dk, (h + 1) * dk)
            q.append(l2n(conv_cols(bi, h * dk)) * (dk ** -0.5))
            k.append(l2n(conv_cols(bi, DN_WIDTH + h * dk)))
            v.append(conv_cols(bi, 2 * DN_WIDTH + h * dk))
            beta.append(beta_d[:, cols])
            gcol.append(g_d[:, cols])
            eg.append(eg_d[:, cols])
            grow.append(gt[DN_HEADS + h:DN_HEADS + h + 1, :])
    ns = range(len(seqs))
    qg = [q[s] * eg[s] for s in ns]
    rhs_all = [jnp.concatenate([v[s] * beta[s], k[s] * (beta[s] * eg[s])], axis=1) for s in ns]

    pairs = [(s, c) for s in ns for c in chunks]
    kq = [_dot(jnp.concatenate([k[s][rows[c]], q[s][rows[c]]], axis=0), k[s][rows[c]].T)
          for s, c in pairs]
    decay = [jnp.where(incl, jnp.exp(jnp.where(incl, gcol[s][rows[c]] - grow[s][:, rows[c]], 0.0)), 0.0)
             for s, c in pairs]
    bmat = [jnp.where(strict, -(beta[s][rows[c]] * kq[i][:chunk] * decay[i]), 0.0)
            for i, (s, c) in enumerate(pairs)]
    qk = [kq[i][chunk:] * decay[i] for i in range(len(pairs))]
    sol = _tri_solve(bmat, [rhs_all[s][rows[c]] for s, c in pairs], chunk, masks)
    g_last = [gcol[s][(c + 1) * chunk - 1:(c + 1) * chunk] for s, c in pairs]
    k_tail = [k[s][rows[c]] * jnp.exp(g_last[i] - gcol[s][rows[c]]) for i, (s, c) in enumerate(pairs)]

    state = [s_scr[bi, h] for bi, h in seqs]
    o = [[] for _ in ns]
    for c in chunks:
        idx = [pairs.index((s, c)) for s in ns]
        ws = [_dot(jnp.concatenate([sol[idx[s]][:, dk:], qg[s][rows[c]]], axis=0), state[s]) for s in ns]
        u = [sol[idx[s]][:, :dk] - ws[s][:chunk] for s in ns]
        for s in ns:
            o[s].append(ws[s][chunk:] + _dot(qk[idx[s]], u[s]))
        state = [state[s] * jnp.exp(g_last[idx[s]]) + _dot(k_tail[idx[s]].T, u[s]) for s in ns]
    for s, (bi, h) in enumerate(seqs):
        s_scr[bi, h] = state[s]
        os_ = jnp.concatenate(o[s], axis=0)
        rms = lax.rsqrt(_lane_sum(os_ * os_) * (1.0 / dk) + EPS)
        y_o[bi, :, h * dk:(h + 1) * dk] = os_ * rms * ng_ref[...] * _silu(z_ref[bi, :, h * dk:(h + 1) * dk])

    @pl.when(t == pl.num_programs(1) - 1)
    def _():
        s_o[...] = s_scr[...]


def _gdn(qkv, z, bg, layer, batch, seq, cw, ng, consts, nb, tt):
    assert GDN_CHUNK == LANES and DN_HEAD_DIM == LANES
    blk = lambda w: pl.BlockSpec((nb, tt, w), lambda p, t: (p, t, 0))
    const3 = lambda a, b_: pl.BlockSpec((None, a, b_), lambda p, t: (layer, 0, 0))
    const2 = lambda a, b_: pl.BlockSpec((a, b_), lambda p, t: (0, 0))
    state = (DN_HEADS, DN_HEAD_DIM, DN_HEAD_DIM)
    seq3 = lambda a: a.reshape(batch, seq, a.shape[-1])
    y, s_new = pl.pallas_call(
        functools.partial(_gdn_kernel, nb=nb, tt=tt, chunk=GDN_CHUNK),
        grid=(batch // nb, seq // tt),
        in_specs=[blk(3 * DN_WIDTH), blk(DN_WIDTH), blk(LANES),
                  const3(DN_CONV, 3 * DN_WIDTH), const3(1, DN_HEAD_DIM),
                  const2(tt, tt), const2(LANES, DN_WIDTH), const2(LANES, DN_WIDTH)],
        out_specs=[blk(DN_WIDTH),
                   pl.BlockSpec((nb,) + state, lambda p, t: (p, 0, 0, 0))],
        out_shape=[jax.ShapeDtypeStruct((batch, seq, DN_WIDTH), F32),
                   jax.ShapeDtypeStruct((batch,) + state, F32)],
        scratch_shapes=[pltpu.VMEM((nb, QKV_HIST + tt, 3 * DN_WIDTH), F32),
                        pltpu.VMEM((nb, DN_CONV - 1, tt, 3 * DN_WIDTH), F32),
                        pltpu.VMEM((nb,) + state, F32)],
        compiler_params=_cparams(("parallel", "arbitrary")),
        name="gated_delta",
    )(seq3(qkv), seq3(z), seq3(bg), cw, ng, *consts)
    return y.reshape(batch * seq, DN_WIDTH), s_new


def _gdn_step_kernel(qkv_ref, sq_ref, z_ref, bg_ref, s_ref, cw_ref, ng_ref, *rest):
    y_o, nq_o, s_o = rest[-3:]
    width = 3 * DN_WIDTH
    qkv = qkv_ref[...]
    bt = qkv.shape[0]
    acc = cw_ref[DN_CONV - 1:DN_CONV, :] * qkv
    for j in range(DN_CONV - 1):
        acc = acc + cw_ref[j:j + 1, :] * sq_ref[:, j, :]
    act = _silu(acc)
    nq_o[:, 0:DN_CONV - 2, :] = sq_ref[:, 1:DN_CONV - 1, :]
    nq_o[:, DN_CONV - 2, :] = qkv
    bg = bg_ref[...]
    for h in range(DN_HEADS):
        cols = slice(h * DN_HEAD_DIM, (h + 1) * DN_HEAD_DIM)
        q = _l2norm(act[:, cols]) * (DN_HEAD_DIM ** -0.5)
        k = _l2norm(act[:, DN_WIDTH + h * DN_HEAD_DIM:DN_WIDTH + (h + 1) * DN_HEAD_DIM])
        v = act[:, 2 * DN_WIDTH + h * DN_HEAD_DIM:2 * DN_WIDTH + (h + 1) * DN_HEAD_DIM]
        beta = bg[:, h:h + 1]
        eg = jnp.exp(bg[:, DN_HEADS + h:DN_HEADS + h + 1])
        qk = jnp.sum(q * k, axis=-1, keepdims=True)
        outs = []
        for b in range(bt):
            s = s_ref[b, h]
            k_col = jnp.broadcast_to(k[b:b + 1], (DN_HEAD_DIM, DN_HEAD_DIM)).T
            q_col = jnp.broadcast_to(q[b:b + 1], (DN_HEAD_DIM, DN_HEAD_DIM)).T
            ks = jnp.sum(k_col * s, axis=0, keepdims=True)
            qs = jnp.sum(q_col * s, axis=0, keepdims=True)
            e = eg[b:b + 1]
            u = beta[b:b + 1] * (v[b:b + 1] - e * ks)
            outs.append(e * qs + qk[b:b + 1] * u)
            s_o[b, h] = e * s + k_col * u
        o = jnp.concatenate(outs, axis=0)
        y_o[:, cols] = _gated_norm(o, ng_ref[...], z_ref[:, cols])


def _carried(prev, first_input, first_output):
    specs = [pl.BlockSpec(memory_space=pl.ANY) for _ in prev]
    return specs, {first_input + j: first_output + j for j in range(len(prev))}


def _gdn_step(qkv, sq_all, z, bg, s_all, prev, layer, cw, ng, bt):
    n = qkv.shape[0]
    row = lambda w: pl.BlockSpec((bt, w), lambda i: (i, 0))
    const3 = lambda a, b_: pl.BlockSpec((None, a, b_), lambda i: (layer, 0, 0))
    sblk = pl.BlockSpec((None, bt, DN_HEADS, DN_HEAD_DIM, DN_HEAD_DIM), lambda i: (layer, i, 0, 0, 0))
    qblk = pl.BlockSpec((None, bt) + sq_all.shape[2:], lambda i: (layer, i, 0, 0))
    carry_specs, aliases = _carried(prev, 7, 1)
    return pl.pallas_call(
        _gdn_step_kernel,
        grid=(n // bt,),
        in_specs=[row(3 * DN_WIDTH), qblk, row(DN_WIDTH), row(LANES), sblk,
                  const3(DN_CONV, 3 * DN_WIDTH), const3(1, DN_HEAD_DIM)] + carry_specs,
        out_specs=[row(DN_WIDTH), qblk, sblk],
        out_shape=[jax.ShapeDtypeStruct(z.shape, F32), jax.ShapeDtypeStruct(sq_all.shape, F32),
                   jax.ShapeDtypeStruct(s_all.shape, F32)],
        input_output_aliases=aliases,
        compiler_params=_cparams(("parallel",)),
        name="gated_delta_step",
    )(qkv, sq_all, z, bg, s_all, cw, ng, *prev)


def _mix_ffn_kernel(x_ref, ya_ref, yb_ref, yc_ref, gate1_ref, shift_ref, scale_ref, gate_ref, g_ref,
                    wo_ref, w1_ref, w2_ref, gf_ref, o_ref, *, final, hidden_chunk):
    a_end, b_end = POOL_WIDTH, POOL_WIDTH + DN_WIDTH
    mix = (_dot(ya_ref[...], wo_ref[0:a_end, :]) + _dot(yb_ref[...], wo_ref[a_end:b_end, :])
           + _dot(yc_ref[...], wo_ref[b_end:, :]))
    x = x_ref[...] + gate1_ref[...] * mix
    h = _mod_rmsnorm(x, g_ref[...], scale_ref[...], shift_ref[...]).astype(BF16)
    acc = jnp.zeros(x.shape, F32)
    for c in range(0, D_FF, hidden_chunk):
        a = jnp.maximum(jnp.dot(h, w1_ref[:, c:c + hidden_chunk], preferred_element_type=F32), 0.0)
        acc = acc + jnp.dot((a * a).astype(BF16), w2_ref[c:c + hidden_chunk, :],
                            preferred_element_type=F32)
    y = x + gate_ref[...] * acc
    if final:
        y = y * lax.rsqrt(jnp.mean(y * y, axis=-1, keepdims=True) + EPS) * gf_ref[...]
    o_ref[...] = y


def _mix_ffn(x, ya, yb, yc, mod, mode, layer, rows_per_batch, g2, w_out, w1, w2, g_final, final, tm):
    n = x.shape[0]
    row = lambda w: pl.BlockSpec((tm, w), lambda i: (i, 0))
    resident = lambda a, b: pl.BlockSpec((None, a, b), lambda i: (layer, 0, 0),
                                         pipeline_mode=pl.Buffered(1))
    return pl.pallas_call(
        functools.partial(_mix_ffn_kernel, final=final, hidden_chunk=1024),
        grid=(n // tm,),
        in_specs=[row(D_MODEL), row(POOL_WIDTH), row(DN_WIDTH), row(CONF_WIDTH),
                  _mod_spec(mode, 2, tm, rows_per_batch),
                  _mod_spec(mode, 3, tm, rows_per_batch),
                  _mod_spec(mode, 4, tm, rows_per_batch),
                  _mod_spec(mode, 5, tm, rows_per_batch),
                  pl.BlockSpec((None, 1, D_MODEL), lambda i: (layer, 0, 0)),
                  resident(D_MODEL, D_MODEL), resident(D_MODEL, D_FF), resident(D_FF, D_MODEL),
                  pl.BlockSpec((1, D_MODEL), lambda i: (0, 0))],
        out_specs=row(D_MODEL),
        out_shape=jax.ShapeDtypeStruct(x.shape, F32),
        compiler_params=_cparams(("parallel",)),
        name="mix_ffn",
    )(x, ya, yb, yc, mod, mod, mod, mod, g2, w_out, w1, w2, g_final)


def kernel(x_prompt, x_sample, state_pool, state_qkv_conv, state_delta, state_conv, c_prompt, c_sample,
           w_ada, b_ada, g_norm1, g_norm2, w_in, pool_w, pool_scale, qkv_conv_w, a_log, dt_bias,
           dn_norm_g, conf_dw_w, conf_dw_b, conf_ln_g, conf_ln_b, conf_pw_w, w_out, w_ff1, w_ff2, g_final):
    depth = w_in.shape[0]
    bp, seq, d = x_prompt.shape
    bs = x_sample.shape[0]
    assert d == D_MODEL and x_sample.shape[1] == 1
    tm_p, nb_gdn, tt_gdn, bt_mix, bt_gdn = 512, GDN_SEQS, GDN_TILE, min(32, bs), min(16, bs)
    assert seq % tm_p == 0 and seq % tt_gdn == 0 and tt_gdn % GDN_CHUNK == 0 and bp % nb_gdn == 0
    assert bs % bt_mix == 0 and bs % bt_gdn == 0 and bs % 8 == 0

    w_in_r = w_in.astype(BF16)
    groups = pool_w.shape[1]
    pw_bd = jnp.einsum("lgcd,gh->lgchd", pool_w, jnp.eye(groups, dtype=pool_w.dtype))
    pw_bd = pw_bd.reshape(depth, POOL_WIDTH, POOL_WIDTH).astype(BF16)
    head_of = jnp.arange(CONF_WIDTH) // (CONF_WIDTH // CONF_HEADS)
    avg = jnp.where(head_of[:, None] == head_of[None, :], CONF_HEADS / CONF_WIDTH, 0.0).astype(BF16)
    tpos = jnp.arange(GDN_TILE)
    ltri = ((tpos[:, None] >= tpos[None, :])
            & (tpos[:, None] // GDN_CHUNK == tpos[None, :] // GDN_CHUNK)).astype(BF16)
    lane_id = jnp.arange(LANES)[:, None]
    head_id = jnp.arange(DN_WIDTH)[None, :] // DN_HEAD_DIM
    gdn_consts = (ltri, (lane_id == head_id).astype(BF16), (lane_id == DN_HEADS + head_id).astype(BF16))
    lane_pad = lambda a: jnp.pad(a, ((0, 0), (DN_HEADS, LANES - 2 * DN_HEADS)))[:, None, :]
    alog_row, dtb_row = lane_pad(a_log), lane_pad(dt_bias)
    r3 = lambda a: a[:, None, :]
    g1, g2, pscale, ng = r3(g_norm1), r3(g_norm2), r3(pool_scale), r3(dn_norm_g)
    cb, lng, lnb = r3(conf_dw_b), r3(conf_ln_g), r3(conf_ln_b)
    cpw = conf_pw_w.astype(BF16)
    w_out_b, w1_b, w2_b = w_out.astype(BF16), w_ff1.astype(BF16), w_ff2.astype(BF16)
    gf = g_final[None, :]

    mods = _ada(jnp.concatenate([c_prompt, c_sample], axis=0), w_ada, b_ada)
    mods_p = mods[:, :bp].reshape(depth, bp * 6, 1, d)
    mods_s = mods[:, bp:]

    xp = x_prompt.reshape(bp * seq, d)
    xs = x_sample.reshape(bs, d)
    pc_weights = (pw_bd, pscale, conf_dw_w, cb, lng, lnb, cpw, avg)
    outs = {0: [], 2: [], 4: [], 6: []}
    sp_all, sc_all, sq_all = state_pool, state_conv, state_qkv_conv
    pc_carry = (jnp.zeros_like(sp_all), jnp.zeros_like(sc_all))
    gdn_carry = (jnp.zeros_like(sq_all), jnp.zeros_like(state_delta))
    for l in range(depth):
        last = l == depth - 1
        mp = mods_p[l]
        u, qkv, z, glu, bg, ya, yc = _inproj(xp, mp, "batch", l, seq, g1, w_in_r, alog_row, dtb_row, tm_p,
                                             poolconf_weights=pc_weights)
        yb, s_new = _gdn(qkv, z, bg, l, bp, seq, qkv_conv_w, ng, gdn_consts, nb_gdn, tt_gdn)
        xp = _mix_ffn(xp, ya, yb, yc, mp, "batch", l, seq, g2, w_out_b, w1_b, w2_b, gf, last, tm_p)
        outs[0].append(u.reshape(bp, seq, -1)[:, seq - POOL_BUF:])
        outs[2].append(qkv.reshape(bp, seq, -1)[:, seq - (DN_CONV - 1):])
        outs[4].append(s_new)
        outs[6].append(glu.reshape(bp, seq, -1)[:, seq - (CONF_K - 1):])
        ms = mods_s[l]
        u, qkv, z, glu, bg = _inproj(xs, ms, "row", l, 1, g1, w_in_r, alog_row, dtb_row, bs)
        ya, yc, *pc_carry = _poolconf_step(u, sp_all, glu, sc_all, pc_carry, l, *pc_weights, bt_mix)
        yb, *gdn_carry = _gdn_step(qkv, sq_all, z, bg, state_delta, gdn_carry, l, qkv_conv_w, ng, bt_gdn)
        xs = _mix_ffn(xs, ya, yb, yc, ms, "row", l, 1, g2, w_out_b, w1_b, w2_b, gf, last, bs)
    n_pool, n_conf = pc_carry
    n_qkv, n_delta = gdn_carry
    return (xp.reshape(bp, seq, d), xs.reshape(bs, 1, d),
            jnp.stack(outs[0]), n_pool, jnp.stack(outs[2]), n_qkv,
            jnp.stack(outs[4]), n_delta, jnp.stack(outs[6]), n_conf)
```

```python
import functools

import jax
import jax.numpy as jnp
from jax import lax
from jax.experimental import pallas as pl
from jax.experimental.pallas import tpu as pltpu

F32 = jnp.float32
BF16 = jnp.bfloat16
EPS = 1e-6

D_MODEL = 1024
POOL_WIDTH = 256
POOL_WINDOWS = (2, 4, 8, 16)
POOL_BUF = 15
DN_WIDTH = 512
DN_HEAD_DIM = 128
DN_HEADS = 4
DN_CONV = 4
CONF_WIDTH = 256
CONF_HEADS = 4
CONF_K = 31
D_FF = 4 * D_MODEL
PAST_LEN = 16384
OFF_QKV = POOL_WIDTH
OFF_Z = OFF_QKV + 3 * DN_WIDTH
OFF_B = OFF_Z + DN_WIDTH
OFF_A = OFF_B + DN_HEADS
OFF_GLU = OFF_A + DN_HEADS
N_IN = OFF_GLU + 2 * CONF_WIDTH

LANES = 128
POOL_HIST = 32
CONF_HIST = 32
QKV_HIST = 8
GDN_CHUNK = 128
GDN_TILE = 128
GDN_SEQS = 4
SOLVE_BASE = 16
VMEM_LIMIT = 56 * 1024 * 1024


def _cparams(sem):
    return pltpu.CompilerParams(dimension_semantics=sem, vmem_limit_bytes=VMEM_LIMIT)


def _dot(a, b):
    return jnp.dot(a.astype(BF16), b.astype(BF16), preferred_element_type=F32)


def _lane_sum(x):
    ones = jnp.ones((x.shape[1], LANES), BF16)
    return jnp.dot(x.astype(BF16), ones, preferred_element_type=F32)


def _split_dot(x, m):
    hi = x.astype(BF16)
    lo = (x - hi.astype(F32)).astype(BF16)
    return (jnp.dot(hi, m, preferred_element_type=F32)
            + jnp.dot(lo, m, preferred_element_type=F32))


def _sigmoid(x):
    return 0.5 * jnp.tanh(0.5 * x) + 0.5


def _silu(x):
    return x * _sigmoid(x)


def _softplus(x):
    return jnp.maximum(x, 0.0) + jnp.log1p(jnp.exp(-jnp.abs(x)))


def _mod_rmsnorm(x, g, scale, shift):
    y = x * lax.rsqrt(jnp.mean(x * x, axis=-1, keepdims=True) + EPS)
    return (y * g) * (1.0 + scale) + shift


def _ada_kernel(c_ref, w_ref, b_ref, o_ref):
    o_ref[...] = _dot(_silu(c_ref[...]), w_ref[...]) + b_ref[...]


def _ada(c_all, w_ada, b_ada):
    depth, d, n = w_ada.shape
    rows = c_all.shape[0]
    tn = 1536
    return pl.pallas_call(
        _ada_kernel,
        grid=(depth, n // tn),
        in_specs=[pl.BlockSpec((rows, d), lambda l, j: (0, 0)),
                  pl.BlockSpec((None, d, tn), lambda l, j: (l, 0, j)),
                  pl.BlockSpec((None, 1, tn), lambda l, j: (l, 0, j))],
        out_specs=pl.BlockSpec((None, rows, tn), lambda l, j: (l, 0, j)),
        out_shape=jax.ShapeDtypeStruct((depth, rows, n), F32),
        compiler_params=_cparams(("arbitrary", "arbitrary")),
        name="ada_mod",
    )(c_all, w_ada, b_ada.reshape(depth, 1, n))


N_POOLCONF_REFS = 8


def _inproj_kernel(x_ref, shift_ref, scale_ref, g_ref, w_ref, alog_ref, dtb_ref, *rest, tm, tiles_per_seq):
    fused = tiles_per_seq is not None
    pc_refs, rest = (rest[:N_POOLCONF_REFS], rest[N_POOLCONF_REFS:]) if fused else ((), rest)
    pool_o, qkv_o, z_o, glu_o, bg_o = rest[:5]
    h = _mod_rmsnorm(x_ref[...], g_ref[...], scale_ref[...], shift_ref[...]).astype(BF16)

    def seg(lo, hi):
        return jnp.dot(h, w_ref[:, lo:hi], preferred_element_type=F32)

    pool_o[...] = seg(0, OFF_QKV)
    tail = seg(OFF_B, N_IN)
    glu_lo = OFF_GLU - OFF_B
    glu_o[...] = (tail[:, glu_lo:glu_lo + CONF_WIDTH]
                  * _sigmoid(tail[:, glu_lo + CONF_WIDTH:glu_lo + 2 * CONF_WIDTH]))
    ba = tail[:, :LANES]
    lane = lax.broadcasted_iota(jnp.int32, ba.shape, 1)
    beta = _sigmoid(ba)
    g = -jnp.exp(alog_ref[...]) * _softplus(ba + dtb_ref[...])
    bg_o[...] = jnp.where(lane < DN_HEADS, beta, g)
    if fused:
        ya_o, yc_o = rest[5:7]
        t = pl.program_id(0) % tiles_per_seq
        ya_o[...], yc_o[...] = _poolconf_tile(t, pool_o, glu_o, *pc_refs, *rest[7:], tt=tm, sub=64)
    qkv_o[...] = seg(OFF_QKV, OFF_Z)
    z_o[...] = seg(OFF_Z, OFF_B)


def _mod_spec(mode, chunk, tm, rows_per_batch):
    if mode == "batch":
        per = rows_per_batch // tm
        return pl.BlockSpec((None, 1, D_MODEL), lambda i: ((i // per) * 6 + chunk, 0, 0))
    return pl.BlockSpec((tm, D_MODEL), lambda i: (i, chunk))


def _inproj(x, mod, mode, layer, rows_per_batch, g1, w_in_r, alog_row, dtb_row, tm, poolconf_weights=None):
    n = x.shape[0]
    row = lambda w: pl.BlockSpec((tm, w), lambda i: (i, 0))
    const3 = lambda a, b: pl.BlockSpec((None, a, b), lambda i: (layer, 0, 0))
    outs = [POOL_WIDTH, 3 * DN_WIDTH, DN_WIDTH, CONF_WIDTH, LANES]
    in_specs = [row(D_MODEL),
                _mod_spec(mode, 0, tm, rows_per_batch),
                _mod_spec(mode, 1, tm, rows_per_batch),
                const3(1, D_MODEL),
                pl.BlockSpec((None, D_MODEL, N_IN), lambda i: (layer, 0, 0), pipeline_mode=pl.Buffered(1)),
                const3(1, LANES), const3(1, LANES)]
    args = (x, mod, mod, g1, w_in_r, alog_row, dtb_row)
    if poolconf_weights is None:
        tiles_per_seq, scratch, sem = None, [], "parallel"
    else:
        tiles_per_seq, scratch, sem = rows_per_batch // tm, _poolconf_scratch(tm), "arbitrary"
        in_specs += [const3(POOL_WIDTH, POOL_WIDTH), const3(1, POOL_WIDTH),
                     const3(CONF_K, CONF_WIDTH), const3(1, CONF_WIDTH),
                     const3(1, CONF_WIDTH), const3(1, CONF_WIDTH), const3(CONF_WIDTH, CONF_WIDTH),
                     pl.BlockSpec((CONF_WIDTH, CONF_WIDTH), lambda i: (0, 0))]
        args += tuple(poolconf_weights)
        outs += [POOL_WIDTH, CONF_WIDTH]
    return pl.pallas_call(
        functools.partial(_inproj_kernel, tm=tm, tiles_per_seq=tiles_per_seq),
        grid=(n // tm,),
        in_specs=in_specs,
        out_specs=[row(w) for w in outs],
        out_shape=[jax.ShapeDtypeStruct((n, w), F32) for w in outs],
        scratch_shapes=scratch,
        compiler_params=_cparams((sem,)),
        name="in_proj",
    )(*args)


def _pool_delta(win, u, pos, half):
    w_lo, w_hi = POOL_WINDOWS[2 * half], POOL_WINDOWS[2 * half + 1]
    a = win(0)
    for j in range(1, w_lo):
        a = a + win(j)
    b = win(w_lo)
    for j in range(w_lo + 1, w_hi):
        b = b + win(j)
    lane = lax.broadcasted_iota(jnp.int32, u.shape, 1)
    upper = lane >= LANES // 2
    s = a + jnp.where(upper, b, 0.0)
    cnt = jnp.where(upper, jnp.minimum(pos + 1, w_hi), jnp.minimum(pos + 1, w_lo)).astype(F32)
    return s / cnt - u


def _conf_post(dc, avg, lng, lnb, cpw):
    mu = _split_dot(dc, avg)
    xc = dc - mu
    var = _split_dot(xc * xc, avg)
    dn = xc * lax.rsqrt(var + EPS) * lng + lnb
    return _dot(_silu(dn), cpw)


def _poolconf_tile(t, u_ref, glu_ref, pw_ref, ps_ref, cw_ref, cb_ref, lng_ref, lnb_ref, cpw_ref,
                   avg_ref, epool, econf, pa, pb, erot, *, tt, sub):
    n = POOL_HIST + tt

    @pl.when(t == 0)
    def _():
        epool[0:POOL_HIST, :] = jnp.zeros((POOL_HIST, POOL_WIDTH), F32)
        econf[0:CONF_HIST, :] = jnp.zeros((CONF_HIST, CONF_WIDTH), F32)

    @pl.when(t > 0)
    def _():
        epool[0:POOL_HIST, :] = epool[tt:tt + POOL_HIST, :]
        econf[0:CONF_HIST, :] = econf[tt:tt + CONF_HIST, :]

    epool[POOL_HIST:, :] = u_ref[...]
    econf[CONF_HIST:, :] = glu_ref[...]

    pos = t * tt + lax.broadcasted_iota(jnp.int32, (tt, 1), 0)
    upper = lax.broadcasted_iota(jnp.int32, (tt, LANES), 1) >= LANES // 2
    halves = []
    for half in range(2):
        cols = slice(half * LANES, (half + 1) * LANES)
        w_lo, w_hi = POOL_WINDOWS[2 * half], POOL_WINDOWS[2 * half + 1]
        pa[8:n, :] = epool[8:n, cols] + epool[7:n - 1, cols]
        pb[16:n, :] = pa[16:n, :] + pa[14:n - 2, :]
        if half == 0:
            s_lo, s_hi = pa[POOL_HIST:n, :], pb[POOL_HIST:n, :]
        else:
            pa[24:n, :] = pb[24:n, :] + pb[20:n - 4, :]
            s_lo = pa[POOL_HIST:n, :]
            s_hi = s_lo + pa[POOL_HIST - 8:n - 8, :]
        cnt = jnp.where(upper, jnp.minimum(pos + 1, w_hi), jnp.minimum(pos + 1, w_lo)).astype(F32)
        halves.append(jnp.where(upper, s_hi, s_lo) / cnt - epool[POOL_HIST:n, cols])
    d = jnp.concatenate(halves, axis=1)
    ya = _dot(d, pw_ref[...]) * ps_ref[...]

    m = CONF_HIST + tt - 8
    for r in range(1, 8):
        erot[r - 1, 0:m, :] = econf[r:r + m, :]
    first = CONF_HIST - (CONF_K - 1)
    dc_blocks = []
    for r0 in range(0, tt, sub):
        acc = cb_ref[...]
        for j in range(CONF_K):
            a8, r = divmod(first + j, 8)
            lo = 8 * a8 + r0
            rows = econf[lo:lo + sub, :] if r == 0 else erot[r - 1, lo:lo + sub, :]
            acc = acc + cw_ref[j:j + 1, :] * rows
        dc_blocks.append(acc)
    dc = jnp.concatenate(dc_blocks, axis=0)
    return ya, _conf_post(dc, avg_ref[...], lng_ref[...], lnb_ref[...], cpw_ref[...])


def _poolconf_scratch(tt):
    return [pltpu.VMEM((POOL_HIST + tt, POOL_WIDTH), F32),
            pltpu.VMEM((CONF_HIST + tt, CONF_WIDTH), F32),
            pltpu.VMEM((POOL_HIST + tt, LANES), F32),
            pltpu.VMEM((POOL_HIST + tt, LANES), F32),
            pltpu.VMEM((7, CONF_HIST + tt, CONF_WIDTH), F32)]


def _poolconf_step_kernel(u_ref, sp_ref, glu_ref, sc_ref, pw_ref, ps_ref, cw_ref, cb_ref, lng_ref,
                          lnb_ref, cpw_ref, avg_ref, *rest, pos0):
    ya_o, yc_o, np_o, nc_o = rest[-4:]
    u = u_ref[...]
    rows = u.shape[0]
    pos = jnp.full((rows, 1), pos0, jnp.int32)
    halves = []
    for half in range(2):
        def win(j, half=half):
            if j == 0:
                return u[:, half * LANES:(half + 1) * LANES]
            return sp_ref[:, POOL_BUF - j, half * LANES:(half + 1) * LANES]
        halves.append(_pool_delta(win, win(0), pos, half))
    d = jnp.concatenate(halves, axis=1)
    ya_o[...] = _dot(d, pw_ref[...]) * ps_ref[...]
    np_o[:, 0:POOL_BUF - 1, :] = sp_ref[:, 1:POOL_BUF, :]
    np_o[:, POOL_BUF - 1, :] = u

    glu = glu_ref[...]
    acc = cb_ref[...] + cw_ref[CONF_K - 1:CONF_K, :] * glu
    for j in range(CONF_K - 1):
        acc = acc + cw_ref[j:j + 1, :] * sc_ref[:, j, :]
    yc_o[...] = _conf_post(acc, avg_ref[...], lng_ref[...], lnb_ref[...], cpw_ref[...])
    nc_o[:, 0:CONF_K - 2, :] = sc_ref[:, 1:CONF_K - 1, :]
    nc_o[:, CONF_K - 2, :] = glu


def _poolconf_step(u, sp_all, glu, sc_all, prev, layer, pw_bd, pscale, cw, cb, lng, lnb, cpw, avg, bt):
    n = u.shape[0]
    row = lambda w: pl.BlockSpec((bt, w), lambda i: (i, 0))
    const3 = lambda a, b_: pl.BlockSpec((None, a, b_), lambda i: (layer, 0, 0))
    state = lambda a: pl.BlockSpec((None, bt) + a.shape[2:], lambda i: (layer, i, 0, 0))
    carry_specs, aliases = _carried(prev, 12, 2)
    return pl.pallas_call(
        functools.partial(_poolconf_step_kernel, pos0=PAST_LEN),
        grid=(n // bt,),
        in_specs=[row(POOL_WIDTH), state(sp_all), row(CONF_WIDTH), state(sc_all),
                  const3(POOL_WIDTH, POOL_WIDTH), const3(1, POOL_WIDTH),
                  const3(CONF_K, CONF_WIDTH), const3(1, CONF_WIDTH),
                  const3(1, CONF_WIDTH), const3(1, CONF_WIDTH),
                  const3(CONF_WIDTH, CONF_WIDTH),
                  pl.BlockSpec((CONF_WIDTH, CONF_WIDTH), lambda i: (0, 0))] + carry_specs,
        out_specs=[row(POOL_WIDTH), row(CONF_WIDTH), state(sp_all), state(sc_all)],
        out_shape=[jax.ShapeDtypeStruct((n, POOL_WIDTH), F32), jax.ShapeDtypeStruct((n, CONF_WIDTH), F32),
                   jax.ShapeDtypeStruct(sp_all.shape, F32), jax.ShapeDtypeStruct(sc_all.shape, F32)],
        input_output_aliases=aliases,
        compiler_params=_cparams(("parallel",)),
        name="pool_conf_step",
    )(u, sp_all, glu, sc_all, pw_bd, pscale, cw, cb, lng, lnb, cpw, avg, *prev)


def _l2norm(x):
    return x * lax.rsqrt(jnp.sum(x * x, axis=-1, keepdims=True) + EPS)


def _gated_norm(o, ng, z):
    return o * lax.rsqrt(jnp.mean(o * o, axis=-1, keepdims=True) + EPS) * ng * _silu(z)


def _tri_masks(c):
    ri = lax.broadcasted_iota(jnp.int32, (c, c), 0)
    ci = lax.broadcasted_iota(jnp.int32, (c, c), 1)
    masks = [ri // SOLVE_BASE == ci // SOLVE_BASE]
    size = SOLVE_BASE
    while size < c:
        masks.append((ri // (2 * size) == ci // (2 * size)) & (ri // size != ci // size))
        size *= 2
    return masks


def _tri_solve(bs, rhss, c, masks):
    ns = [jnp.where(masks[0], b, 0.0) for b in bs]
    ps = [_dot(n, n) for n in ns]
    power = 2
    while power < SOLVE_BASE:
        if 2 * power < SOLVE_BASE:
            sts = [_dot(jnp.concatenate([n, p], axis=0), p) for n, p in zip(ns, ps)]
            ns = [n + p + st[:c] for n, p, st in zip(ns, ps, sts)]
            ps = [st[c:] for st in sts]
        else:
            ns = [n + p + _dot(n, p) for n, p in zip(ns, ps)]
        power *= 2
    for m in masks[1:]:
        ls = [jnp.where(m, b, 0.0) for b in bs]
        tls = [l + _dot(n, l) for n, l in zip(ns, ls)]
        ns = [n + tl + _dot(tl, n) for n, tl in zip(ns, tls)]
    return [rhs + _dot(n, rhs) for n, rhs in zip(ns, rhss)]


def _gdn_kernel(qkv_ref, z_ref, bg_ref, cw_ref, ng_ref, ltri_ref, selb_ref, selg_ref, y_o, s_o,
                eq, eqs, s_scr, *, nb, tt, chunk):
    t = pl.program_id(1)
    width = 3 * DN_WIDTH
    dk = DN_HEAD_DIM

    @pl.when(t == 0)
    def _():
        eq[:, 0:QKV_HIST, :] = jnp.zeros((nb, QKV_HIST, width), F32)
        s_scr[...] = jnp.zeros(s_scr.shape, F32)

    @pl.when(t > 0)
    def _():
        eq[:, 0:QKV_HIST, :] = eq[:, tt:tt + QKV_HIST, :]

    eq[:, QKV_HIST:, :] = qkv_ref[...]

    for r in range(1, DN_CONV):
        eqs[:, r - 1] = eq[:, QKV_HIST - r:QKV_HIST - r + tt, :]

    def conv_cols(bi, lo):
        acc = cw_ref[DN_CONV - 1:DN_CONV, lo:lo + LANES] * eq[bi, QKV_HIST:QKV_HIST + tt, lo:lo + LANES]
        for r in range(1, DN_CONV):
            acc = acc + cw_ref[DN_CONV - 1 - r:DN_CONV - r, lo:lo + LANES] * eqs[bi, r - 1, :, lo:lo + LANES]
        return _silu(acc)

    def split3(x):
        hi = x.astype(BF16)
        r = x - hi.astype(F32)
        mid = r.astype(BF16)
        return hi, mid, (r - mid.astype(F32)).astype(BF16)

    def dot3(a, pieces, left):
        return sum(jnp.dot(a, p, preferred_element_type=F32) if left else
                   jnp.dot(p, a, preferred_element_type=F32) for p in pieces)

    def l2n(x):
        return x * lax.rsqrt(_lane_sum(x * x) + EPS)

    rc = lax.broadcasted_iota(jnp.int32, (chunk, chunk), 0)
    cc = lax.broadcasted_iota(jnp.int32, (chunk, chunk), 1)
    incl = rc >= cc
    strict = rc > cc
    masks = _tri_masks(chunk)
    chunks = range(tt // chunk)
    rows = [slice(c * chunk, (c + 1) * chunk) for c in chunks]

    seqs = [(bi, h) for bi in range(nb) for h in range(DN_HEADS)]
    q, k, v, beta, gcol, grow, eg = [], [], [], [], [], [], []
    for bi in range(nb):
        bg3 = split3(bg_ref[bi])
        gc = dot3(ltri_ref[...], bg3, True)
        gt = gc.T
        beta_d = dot3(selb_ref[...], bg3, False)
        g_d = dot3(selg_ref[...], split3(gc), False)
        eg_d = jnp.exp(g_d)
        for h in range(DN_HEADS):
            cols = slice(h * dk, (h + 1) * dk)
            q.append(l2n(conv_cols(bi, h * dk)) * (dk ** -0.5))
            k.append(l2n(conv_cols(bi, DN_WIDTH + h * dk)))
            v.append(conv_cols(bi, 2 * DN_WIDTH + h * dk))
            beta.append(beta_d[:, cols])
            gcol.append(g_d[:, cols])
            eg.append(eg_d[:, cols])
            grow.append(gt[DN_HEADS + h:DN_HEADS + h + 1, :])
    ns = range(len(seqs))
    qg = [q[s] * eg[s] for s in ns]
    rhs_all = [jnp.concatenate([v[s] * beta[s], k[s] * (beta[s] * eg[s])], axis=1) for s in ns]

    pairs = [(s, c) for s in ns for c in chunks]
    kq = [_dot(jnp.concatenate([k[s][rows[c]], q[s][rows[c]]], axis=0), k[s][rows[c]].T)
          for s, c in pairs]
    decay = [jnp.where(incl, jnp.exp(jnp.where(incl, gcol[s][rows[c]] - grow[s][:, rows[c]], 0.0)), 0.0)
             for s, c in pairs]
    bmat = [jnp.where(strict, -(beta[s][rows[c]] * kq[i][:chunk] * decay[i]), 0.0)
            for i, (s, c) in enumerate(pairs)]
    qk = [kq[i][chunk:] * decay[i] for i in range(len(pairs))]
    sol = _tri_solve(bmat, [rhs_all[s][rows[c]] for s, c in pairs], chunk, masks)
    g_last = [gcol[s][(c + 1) * chunk - 1:(c + 1) * chunk] for s, c in pairs]
    k_tail = [k[s][rows[c]] * jnp.exp(g_last[i] - gcol[s][rows[c]]) for i, (s, c) in enumerate(pairs)]

    state = [s_scr[bi, h] for bi, h in seqs]
    o = [[] for _ in ns]
    for c in chunks:
        idx = [pairs.index((s, c)) for s in ns]
        ws = [_dot(jnp.concatenate([sol[idx[s]][:, dk:], qg[s][rows[c]]], axis=0), state[s]) for s in ns]
        u = [sol[idx[s]][:, :dk] - ws[s][:chunk] for s in ns]
        for s in ns:
            o[s].append(ws[s][chunk:] + _dot(qk[idx[s]], u[s]))
        state = [state[s] * jnp.exp(g_last[idx[s]]) + _dot(k_tail[idx[s]].T, u[s]) for s in ns]
    for s, (bi, h) in enumerate(seqs):
        s_scr[bi, h] = state[s]
        os_ = jnp.concatenate(o[s], axis=0)
        rms = lax.rsqrt(_lane_sum(os_ * os_) * (1.0 / dk) + EPS)
        y_o[bi, :, h * dk:(h + 1) * dk] = os_ * rms * ng_ref[...] * _silu(z_ref[bi, :, h * dk:(h + 1) * dk])

    @pl.when(t == pl.num_programs(1) - 1)
    def _():
        s_o[...] = s_scr[...]


def _gdn(qkv, z, bg, layer, batch, seq, cw, ng, consts, nb, tt):
    assert GDN_CHUNK == LANES and DN_HEAD_DIM == LANES
    blk = lambda w: pl.BlockSpec((nb, tt, w), lambda p, t: (p, t, 0))
    const3 = lambda a, b_: pl.BlockSpec((None, a, b_), lambda p, t: (layer, 0, 0))
    const2 = lambda a, b_: pl.BlockSpec((a, b_), lambda p, t: (0, 0))
    state = (DN_HEADS, DN_HEAD_DIM, DN_HEAD_DIM)
    seq3 = lambda a: a.reshape(batch, seq, a.shape[-1])
    y, s_new = pl.pallas_call(
        functools.partial(_gdn_kernel, nb=nb, tt=tt, chunk=GDN_CHUNK),
        grid=(batch // nb, seq // tt),
        in_specs=[blk(3 * DN_WIDTH), blk(DN_WIDTH), blk(LANES),
                  const3(DN_CONV, 3 * DN_WIDTH), const3(1, DN_HEAD_DIM),
                  const2(tt, tt), const2(LANES, DN_WIDTH), const2(LANES, DN_WIDTH)],
        out_specs=[blk(DN_WIDTH),
                   pl.BlockSpec((nb,) + state, lambda p, t: (p, 0, 0, 0))],
        out_shape=[jax.ShapeDtypeStruct((batch, seq, DN_WIDTH), F32),
                   jax.ShapeDtypeStruct((batch,) + state, F32)],
        scratch_shapes=[pltpu.VMEM((nb, QKV_HIST + tt, 3 * DN_WIDTH), F32),
                        pltpu.VMEM((nb, DN_CONV - 1, tt, 3 * DN_WIDTH), F32),
                        pltpu.VMEM((nb,) + state, F32)],
        compiler_params=_cparams(("parallel", "arbitrary")),
        name="gated_delta",
    )(seq3(qkv), seq3(z), seq3(bg), cw, ng, *consts)
    return y.reshape(batch * seq, DN_WIDTH), s_new


def _gdn_step_kernel(qkv_ref, sq_ref, z_ref, bg_ref, s_ref, cw_ref, ng_ref, *rest):
    y_o, nq_o, s_o = rest[-3:]
    width = 3 * DN_WIDTH
    qkv = qkv_ref[...]
    bt = qkv.shape[0]
    acc = cw_ref[DN_CONV - 1:DN_CONV, :] * qkv
    for j in range(DN_CONV - 1):
        acc = acc + cw_ref[j:j + 1, :] * sq_ref[:, j, :]
    act = _silu(acc)
    nq_o[:, 0:DN_CONV - 2, :] = sq_ref[:, 1:DN_CONV - 1, :]
    nq_o[:, DN_CONV - 2, :] = qkv
    bg = bg_ref[...]
    for h in range(DN_HEADS):
        cols = slice(h * DN_HEAD_DIM, (h + 1) * DN_HEAD_DIM)
        q = _l2norm(act[:, cols]) * (DN_HEAD_DIM ** -0.5)
        k = _l2norm(act[:, DN_WIDTH + h * DN_HEAD_DIM:DN_WIDTH + (h + 1) * DN_HEAD_DIM])
        v = act[:, 2 * DN_WIDTH + h * DN_HEAD_DIM:2 * DN_WIDTH + (h + 1) * DN_HEAD_DIM]
        beta = bg[:, h:h + 1]
        eg = jnp.exp(bg[:, DN_HEADS + h:DN_HEADS + h + 1])
        qk = jnp.sum(q * k, axis=-1, keepdims=True)
        outs = []
        for b in range(bt):
            s = s_ref[b, h]
            k_col = jnp.broadcast_to(k[b:b + 1], (DN_HEAD_DIM, DN_HEAD_DIM)).T
            q_col = jnp.broadcast_to(q[b:b + 1], (DN_HEAD_DIM, DN_HEAD_DIM)).T
            ks = jnp.sum(k_col * s, axis=0, keepdims=True)
            qs = jnp.sum(q_col * s, axis=0, keepdims=True)
            e = eg[b:b + 1]
            u = beta[b:b + 1] * (v[b:b + 1] - e * ks)
            outs.append(e * qs + qk[b:b + 1] * u)
            s_o[b, h] = e * s + k_col * u
        o = jnp.concatenate(outs, axis=0)
        y_o[:, cols] = _gated_norm(o, ng_ref[...], z_ref[:, cols])


def _carried(prev, first_input, first_output):
    specs = [pl.BlockSpec(memory_space=pl.ANY) for _ in prev]
    return specs, {first_input + j: first_output + j for j in range(len(prev))}


def _gdn_step(qkv, sq_all, z, bg, s_all, prev, layer, cw, ng, bt):
    n = qkv.shape[0]
    row = lambda w: pl.BlockSpec((bt, w), lambda i: (i, 0))
    const3 = lambda a, b_: pl.BlockSpec((None, a, b_), lambda i: (layer, 0, 0))
    sblk = pl.BlockSpec((None, bt, DN_HEADS, DN_HEAD_DIM, DN_HEAD_DIM), lambda i: (layer, i, 0, 0, 0))
    qblk = pl.BlockSpec((None, bt) + sq_all.shape[2:], lambda i: (layer, i, 0, 0))
    carry_specs, aliases = _carried(prev, 7, 1)
    return pl.pallas_call(
        _gdn_step_kernel,
        grid=(n // bt,),
        in_specs=[row(3 * DN_WIDTH), qblk, row(DN_WIDTH), row(LANES), sblk,
                  const3(DN_CONV, 3 * DN_WIDTH), const3(1, DN_HEAD_DIM)] + carry_specs,
        out_specs=[row(DN_WIDTH), qblk, sblk],
        out_shape=[jax.ShapeDtypeStruct(z.shape, F32), jax.ShapeDtypeStruct(sq_all.shape, F32),
                   jax.ShapeDtypeStruct(s_all.shape, F32)],
        input_output_aliases=aliases,
        compiler_params=_cparams(("parallel",)),
        name="gated_delta_step",
    )(qkv, sq_all, z, bg, s_all, cw, ng, *prev)


def _mix_ffn_kernel(x_ref, ya_ref, yb_ref, yc_ref, gate1_ref, shift_ref, scale_ref, gate_ref, g_ref,
                    wo_ref, w1_ref, w2_ref, gf_ref, o_ref, *, final, hidden_chunk):
    a_end, b_end = POOL_WIDTH, POOL_WIDTH + DN_WIDTH
    mix = (_dot(ya_ref[...], wo_ref[0:a_end, :]) + _dot(yb_ref[...], wo_ref[a_end:b_end, :])
           + _dot(yc_ref[...], wo_ref[b_end:, :]))
    x = x_ref[...] + gate1_ref[...] * mix
    h = _mod_rmsnorm(x, g_ref[...], scale_ref[...], shift_ref[...]).astype(BF16)
    acc = jnp.zeros(x.shape, F32)
    for c in range(0, D_FF, hidden_chunk):
        a = jnp.maximum(jnp.dot(h, w1_ref[:, c:c + hidden_chunk], preferred_element_type=F32), 0.0)
        acc = acc + jnp.dot((a * a).astype(BF16), w2_ref[c:c + hidden_chunk, :],
                            preferred_element_type=F32)
    y = x + gate_ref[...] * acc
    if final:
        y = y * lax.rsqrt(jnp.mean(y * y, axis=-1, keepdims=True) + EPS) * gf_ref[...]
    o_ref[...] = y


def _mix_ffn(x, ya, yb, yc, mod, mode, layer, rows_per_batch, g2, w_out, w1, w2, g_final, final, tm):
    n = x.shape[0]
    row = lambda w: pl.BlockSpec((tm, w), lambda i: (i, 0))
    resident = lambda a, b: pl.BlockSpec((None, a, b), lambda i: (layer, 0, 0),
                                         pipeline_mode=pl.Buffered(1))
    return pl.pallas_call(
        functools.partial(_mix_ffn_kernel, final=final, hidden_chunk=1024),
        grid=(n // tm,),
        in_specs=[row(D_MODEL), row(POOL_WIDTH), row(DN_WIDTH), row(CONF_WIDTH),
                  _mod_spec(mode, 2, tm, rows_per_batch),
                  _mod_spec(mode, 3, tm, rows_per_batch),
                  _mod_spec(mode, 4, tm, rows_per_batch),
                  _mod_spec(mode, 5, tm, rows_per_batch),
                  pl.BlockSpec((None, 1, D_MODEL), lambda i: (layer, 0, 0)),
                  resident(D_MODEL, D_MODEL), resident(D_MODEL, D_FF), resident(D_FF, D_MODEL),
                  pl.BlockSpec((1, D_MODEL), lambda i: (0, 0))],
        out_specs=row(D_MODEL),
        out_shape=jax.ShapeDtypeStruct(x.shape, F32),
        compiler_params=_cparams(("parallel",)),
        name="mix_ffn",
    )(x, ya, yb, yc, mod, mod, mod, mod, g2, w_out, w1, w2, g_final)


def kernel(x_prompt, x_sample, state_pool, state_qkv_conv, state_delta, state_conv, c_prompt, c_sample,
           w_ada, b_ada, g_norm1, g_norm2, w_in, pool_w, pool_scale, qkv_conv_w, a_log, dt_bias,
           dn_norm_g, conf_dw_w, conf_dw_b, conf_ln_g, conf_ln_b, conf_pw_w, w_out, w_ff1, w_ff2, g_final):
    depth = w_in.shape[0]
    bp, seq, d = x_prompt.shape
    bs = x_sample.shape[0]
    assert d == D_MODEL and x_sample.shape[1] == 1
    tm_p, nb_gdn, tt_gdn, bt_mix, bt_gdn = 512, GDN_SEQS, GDN_TILE, min(32, bs), min(16, bs)
    assert seq % tm_p == 0 and seq % tt_gdn == 0 and tt_gdn % GDN_CHUNK == 0 and bp % nb_gdn == 0
    assert bs % bt_mix == 0 and bs % bt_gdn == 0 and bs % 8 == 0

    w_in_r = w_in.astype(BF16)
    groups = pool_w.shape[1]
    pw_bd = jnp.einsum("lgcd,gh->lgchd", pool_w, jnp.eye(groups, dtype=pool_w.dtype))
    pw_bd = pw_bd.reshape(depth, POOL_WIDTH, POOL_WIDTH).astype(BF16)
    head_of = jnp.arange(CONF_WIDTH) // (CONF_WIDTH // CONF_HEADS)
    avg = jnp.where(head_of[:, None] == head_of[None, :], CONF_HEADS / CONF_WIDTH, 0.0).astype(BF16)
    tpos = jnp.arange(GDN_TILE)
    ltri = ((tpos[:, None] >= tpos[None, :])
            & (tpos[:, None] // GDN_CHUNK == tpos[None, :] // GDN_CHUNK)).astype(BF16)
    lane_id = jnp.arange(LANES)[:, None]
    head_id = jnp.arange(DN_WIDTH)[None, :] // DN_HEAD_DIM
    gdn_consts = (ltri, (lane_id == head_id).astype(BF16), (lane_id == DN_HEADS + head_id).astype(BF16))
    lane_pad = lambda a: jnp.pad(a, ((0, 0), (DN_HEADS, LANES - 2 * DN_HEADS)))[:, None, :]
    alog_row, dtb_row = lane_pad(a_log), lane_pad(dt_bias)
    r3 = lambda a: a[:, None, :]
    g1, g2, pscale, ng = r3(g_norm1), r3(g_norm2), r3(pool_scale), r3(dn_norm_g)
    cb, lng, lnb = r3(conf_dw_b), r3(conf_ln_g), r3(conf_ln_b)
    cpw = conf_pw_w.astype(BF16)
    w_out_b, w1_b, w2_b = w_out.astype(BF16), w_ff1.astype(BF16), w_ff2.astype(BF16)
    gf = g_final[None, :]

    mods = _ada(jnp.concatenate([c_prompt, c_sample], axis=0), w_ada, b_ada)
    mods_p = mods[:, :bp].reshape(depth, bp * 6, 1, d)
    mods_s = mods[:, bp:]

    xp = x_prompt.reshape(bp * seq, d)
    xs = x_sample.reshape(bs, d)
    pc_weights = (pw_bd, pscale, conf_dw_w, cb, lng, lnb, cpw, avg)
    outs = {0: [], 2: [], 4: [], 6: []}
    sp_all, sc_all, sq_all = state_pool, state_conv, state_qkv_conv
    pc_carry = (jnp.zeros_like(sp_all), jnp.zeros_like(sc_all))
    gdn_carry = (jnp.zeros_like(sq_all), jnp.zeros_like(state_delta))
    for l in range(depth):
        last = l == depth - 1
        mp = mods_p[l]
        u, qkv, z, glu, bg, ya, yc = _inproj(xp, mp, "batch", l, seq, g1, w_in_r, alog_row, dtb_row, tm_p,
                                             poolconf_weights=pc_weights)
        yb, s_new = _gdn(qkv, z, bg, l, bp, seq, qkv_conv_w, ng, gdn_consts, nb_gdn, tt_gdn)
        xp = _mix_ffn(xp, ya, yb, yc, mp, "batch", l, seq, g2, w_out_b, w1_b, w2_b, gf, last, tm_p)
        outs[0].append(u.reshape(bp, seq, -1)[:, seq - POOL_BUF:])
        outs[2].append(qkv.reshape(bp, seq, -1)[:, seq - (DN_CONV - 1):])
        outs[4].append(s_new)
        outs[6].append(glu.reshape(bp, seq, -1)[:, seq - (CONF_K - 1):])
        ms = mods_s[l]
        u, qkv, z, glu, bg = _inproj(xs, ms, "row", l, 1, g1, w_in_r, alog_row, dtb_row, bs)
        ya, yc, *pc_carry = _poolconf_step(u, sp_all, glu, sc_all, pc_carry, l, *pc_weights, bt_mix)
        yb, *gdn_carry = _gdn_step(qkv, sq_all, z, bg, state_delta, gdn_carry, l, qkv_conv_w, ng, bt_gdn)
        xs = _mix_ffn(xs, ya, yb, yc, ms, "row", l, 1, g2, w_out_b, w1_b, w2_b, gf, last, bs)
    n_pool, n_conf = pc_carry
    n_qkv, n_delta = gdn_carry
    return (xp.reshape(bp, seq, d), xs.reshape(bs, 1, d),
            jnp.stack(outs[0]), n_pool, jnp.stack(outs[2]), n_qkv,
            jnp.stack(outs[4]), n_delta, jnp.stack(outs[6]), n_conf)
```

```python
import functools

import jax
import jax.numpy as jnp
from jax import lax
from jax.experimental import pallas as pl
from jax.experimental.pallas import tpu as pltpu

F32 = jnp.float32
BF16 = jnp.bfloat16
EPS = 1e-6

D_MODEL = 1024
POOL_WIDTH = 256
POOL_WINDOWS = (2, 4, 8, 16)
POOL_BUF = 15
DN_WIDTH = 512
DN_HEAD_DIM = 128
DN_HEADS = 4
DN_CONV = 4
CONF_WIDTH = 256
CONF_HEADS = 4
CONF_K = 31
D_FF = 4 * D_MODEL
PAST_LEN = 16384
OFF_QKV = POOL_WIDTH
OFF_Z = OFF_QKV + 3 * DN_WIDTH
OFF_B = OFF_Z + DN_WIDTH
OFF_A = OFF_B + DN_HEADS
OFF_GLU = OFF_A + DN_HEADS
N_IN = OFF_GLU + 2 * CONF_WIDTH

LANES = 128
POOL_HIST = 32
CONF_HIST = 32
QKV_HIST = 8
GDN_CHUNK = 128
GDN_TILE = 128
GDN_SEQS = 4
SOLVE_BASE = 16
VMEM_LIMIT = 56 * 1024 * 1024


def _cparams(sem):
    return pltpu.CompilerParams(dimension_semantics=sem, vmem_limit_bytes=VMEM_LIMIT)


def _dot(a, b):
    return jnp.dot(a.astype(BF16), b.astype(BF16), preferred_element_type=F32)


def _lane_sum(x):
    ones = jnp.ones((x.shape[1], LANES), BF16)
    return jnp.dot(x.astype(BF16), ones, preferred_element_type=F32)


def _split_dot(x, m):
    hi = x.astype(BF16)
    lo = (x - hi.astype(F32)).astype(BF16)
    return (jnp.dot(hi, m, preferred_element_type=F32)
            + jnp.dot(lo, m, preferred_element_type=F32))


def _sigmoid(x):
    return 0.5 * jnp.tanh(0.5 * x) + 0.5


def _silu(x):
    return x * _sigmoid(x)


def _softplus(x):
    return jnp.maximum(x, 0.0) + jnp.log1p(jnp.exp(-jnp.abs(x)))


def _mod_rmsnorm(x, g, scale, shift):
    y = x * lax.rsqrt(jnp.mean(x * x, axis=-1, keepdims=True) + EPS)
    return (y * g) * (1.0 + scale) + shift


def _ada_kernel(c_ref, w_ref, b_ref, o_ref):
    o_ref[...] = _dot(_silu(c_ref[...]), w_ref[...]) + b_ref[...]


def _ada(c_all, w_ada, b_ada):
    depth, d, n = w_ada.shape
    rows = c_all.shape[0]
    tn = 1536
    return pl.pallas_call(
        _ada_kernel,
        grid=(depth, n // tn),
        in_specs=[pl.BlockSpec((rows, d), lambda l, j: (0, 0)),
                  pl.BlockSpec((None, d, tn), lambda l, j: (l, 0, j)),
                  pl.BlockSpec((None, 1, tn), lambda l, j: (l, 0, j))],
        out_specs=pl.BlockSpec((None, rows, tn), lambda l, j: (l, 0, j)),
        out_shape=jax.ShapeDtypeStruct((depth, rows, n), F32),
        compiler_params=_cparams(("arbitrary", "arbitrary")),
        name="ada_mod",
    )(c_all, w_ada, b_ada.reshape(depth, 1, n))


N_POOLCONF_REFS = 8


def _inproj_kernel(x_ref, shift_ref, scale_ref, g_ref, w_ref, alog_ref, dtb_ref, *rest, tm, tiles_per_seq):
    fused = tiles_per_seq is not None
    pc_refs, rest = (rest[:N_POOLCONF_REFS], rest[N_POOLCONF_REFS:]) if fused else ((), rest)
    pool_o, qkv_o, z_o, glu_o, bg_o = rest[:5]
    h = _mod_rmsnorm(x_ref[...], g_ref[...], scale_ref[...], shift_ref[...]).astype(BF16)

    def seg(lo, hi):
        return jnp.dot(h, w_ref[:, lo:hi], preferred_element_type=F32)

    pool_o[...] = seg(0, OFF_QKV)
    tail = seg(OFF_B, N_IN)
    glu_lo = OFF_GLU - OFF_B
    glu_o[...] = (tail[:, glu_lo:glu_lo + CONF_WIDTH]
                  * _sigmoid(tail[:, glu_lo + CONF_WIDTH:glu_lo + 2 * CONF_WIDTH]))
    ba = tail[:, :LANES]
    lane = lax.broadcasted_iota(jnp.int32, ba.shape, 1)
    beta = _sigmoid(ba)
    g = -jnp.exp(alog_ref[...]) * _softplus(ba + dtb_ref[...])
    bg_o[...] = jnp.where(lane < DN_HEADS, beta, g)
    if fused:
        ya_o, yc_o = rest[5:7]
        t = pl.program_id(0) % tiles_per_seq
        ya_o[...], yc_o[...] = _poolconf_tile(t, pool_o, glu_o, *pc_refs, *rest[7:], tt=tm, sub=64)
    half = tm // 2 if tm >= 256 else tm
    for r in range(0, tm, half):
        hs = h[r:r + half]
        qkv_o[r:r + half, :] = jnp.dot(hs, w_ref[:, OFF_QKV:OFF_Z], preferred_element_type=F32)
        z_o[r:r + half, :] = jnp.dot(hs, w_ref[:, OFF_Z:OFF_B], preferred_element_type=F32)


def _mod_spec(mode, chunk, tm, rows_per_batch):
    if mode == "batch":
        per = rows_per_batch // tm
        return pl.BlockSpec((None, 1, D_MODEL), lambda i: ((i // per) * 6 + chunk, 0, 0))
    return pl.BlockSpec((tm, D_MODEL), lambda i: (i, chunk))


def _inproj(x, mod, mode, layer, rows_per_batch, g1, w_in_r, alog_row, dtb_row, tm, poolconf_weights=None):
    n = x.shape[0]
    row = lambda w: pl.BlockSpec((tm, w), lambda i: (i, 0))
    const3 = lambda a, b: pl.BlockSpec((None, a, b), lambda i: (layer, 0, 0))
    outs = [POOL_WIDTH, 3 * DN_WIDTH, DN_WIDTH, CONF_WIDTH, LANES]
    in_specs = [row(D_MODEL),
                _mod_spec(mode, 0, tm, rows_per_batch),
                _mod_spec(mode, 1, tm, rows_per_batch),
                const3(1, D_MODEL),
                pl.BlockSpec((None, D_MODEL, N_IN), lambda i: (layer, 0, 0), pipeline_mode=pl.Buffered(1)),
                const3(1, LANES), const3(1, LANES)]
    args = (x, mod, mod, g1, w_in_r, alog_row, dtb_row)
    if poolconf_weights is None:
        tiles_per_seq, scratch, sem = None, [], "parallel"
    else:
        tiles_per_seq, scratch, sem = rows_per_batch // tm, _poolconf_scratch(tm), "arbitrary"
        in_specs += [const3(POOL_WIDTH, POOL_WIDTH), const3(1, POOL_WIDTH),
                     const3(CONF_K, CONF_WIDTH), const3(1, CONF_WIDTH),
                     const3(1, CONF_WIDTH), const3(1, CONF_WIDTH), const3(CONF_WIDTH, CONF_WIDTH),
                     pl.BlockSpec((CONF_WIDTH, CONF_WIDTH), lambda i: (0, 0))]
        args += tuple(poolconf_weights)
        outs += [POOL_WIDTH, CONF_WIDTH]
    return pl.pallas_call(
        functools.partial(_inproj_kernel, tm=tm, tiles_per_seq=tiles_per_seq),
        grid=(n // tm,),
        in_specs=in_specs,
        out_specs=[row(w) for w in outs],
        out_shape=[jax.ShapeDtypeStruct((n, w), F32) for w in outs],
        scratch_shapes=scratch,
        compiler_params=_cparams((sem,)),
        name="in_proj",
    )(*args)


def _pool_delta(win, u, pos, half):
    w_lo, w_hi = POOL_WINDOWS[2 * half], POOL_WINDOWS[2 * half + 1]
    a = win(0)
    for j in range(1, w_lo):
        a = a + win(j)
    b = win(w_lo)
    for j in range(w_lo + 1, w_hi):
        b = b + win(j)
    lane = lax.broadcasted_iota(jnp.int32, u.shape, 1)
    upper = lane >= LANES // 2
    s = a + jnp.where(upper, b, 0.0)
    cnt = jnp.where(upper, jnp.minimum(pos + 1, w_hi), jnp.minimum(pos + 1, w_lo)).astype(F32)
    return s / cnt - u


def _conf_post(dc, avg, lng, lnb, cpw):
    mu = _split_dot(dc, avg)
    xc = dc - mu
    var = _split_dot(xc * xc, avg)
    dn = xc * lax.rsqrt(var + EPS) * lng + lnb
    return _dot(_silu(dn), cpw)


def _poolconf_tile(t, u_ref, glu_ref, pw_ref, ps_ref, cw_ref, cb_ref, lng_ref, lnb_ref, cpw_ref,
                   avg_ref, epool, econf, pa, pb, erot, *, tt, sub):
    n = POOL_HIST + tt

    @pl.when(t == 0)
    def _():
        epool[0:POOL_HIST, :] = jnp.zeros((POOL_HIST, POOL_WIDTH), F32)
        econf[0:CONF_HIST, :] = jnp.zeros((CONF_HIST, CONF_WIDTH), F32)

    @pl.when(t > 0)
    def _():
        epool[0:POOL_HIST, :] = epool[tt:tt + POOL_HIST, :]
        econf[0:CONF_HIST, :] = econf[tt:tt + CONF_HIST, :]

    epool[POOL_HIST:, :] = u_ref[...]
    econf[CONF_HIST:, :] = glu_ref[...]

    pos = t * tt + lax.broadcasted_iota(jnp.int32, (tt, 1), 0)
    upper = lax.broadcasted_iota(jnp.int32, (tt, LANES), 1) >= LANES // 2
    halves = []
    for half in range(2):
        cols = slice(half * LANES, (half + 1) * LANES)
        w_lo, w_hi = POOL_WINDOWS[2 * half], POOL_WINDOWS[2 * half + 1]
        pa[8:n, :] = epool[8:n, cols] + epool[7:n - 1, cols]
        pb[16:n, :] = pa[16:n, :] + pa[14:n - 2, :]
        if half == 0:
            s_lo, s_hi = pa[POOL_HIST:n, :], pb[POOL_HIST:n, :]
        else:
            pa[24:n, :] = pb[24:n, :] + pb[20:n - 4, :]
            s_lo = pa[POOL_HIST:n, :]
            s_hi = s_lo + pa[POOL_HIST - 8:n - 8, :]
        cnt = jnp.where(upper, jnp.minimum(pos + 1, w_hi), jnp.minimum(pos + 1, w_lo)).astype(F32)
        halves.append(jnp.where(upper, s_hi, s_lo) / cnt - epool[POOL_HIST:n, cols])
    d = jnp.concatenate(halves, axis=1)
    ya = _dot(d, pw_ref[...]) * ps_ref[...]

    m = CONF_HIST + tt - 8
    for r in range(1, 8):
        erot[r - 1, 0:m, :] = econf[r:r + m, :]
    first = CONF_HIST - (CONF_K - 1)
    dc_blocks = []
    for r0 in range(0, tt, sub):
        acc = cb_ref[...]
        for j in range(CONF_K):
            a8, r = divmod(first + j, 8)
            lo = 8 * a8 + r0
            rows = econf[lo:lo + sub, :] if r == 0 else erot[r - 1, lo:lo + sub, :]
            acc = acc + cw_ref[j:j + 1, :] * rows
        dc_blocks.append(acc)
    dc = jnp.concatenate(dc_blocks, axis=0)
    return ya, _conf_post(dc, avg_ref[...], lng_ref[...], lnb_ref[...], cpw_ref[...])


def _poolconf_scratch(tt):
    return [pltpu.VMEM((POOL_HIST + tt, POOL_WIDTH), F32),
            pltpu.VMEM((CONF_HIST + tt, CONF_WIDTH), F32),
            pltpu.VMEM((POOL_HIST + tt, LANES), F32),
            pltpu.VMEM((POOL_HIST + tt, LANES), F32),
            pltpu.VMEM((7, CONF_HIST + tt, CONF_WIDTH), F32)]


def _poolconf_step_kernel(u_ref, sp_ref, glu_ref, sc_ref, pw_ref, ps_ref, cw_ref, cb_ref, lng_ref,
                          lnb_ref, cpw_ref, avg_ref, *rest, pos0):
    ya_o, yc_o, np_o, nc_o = rest[-4:]
    u = u_ref[...]
    rows = u.shape[0]
    pos = jnp.full((rows, 1), pos0, jnp.int32)
    halves = []
    for half in range(2):
        def win(j, half=half):
            if j == 0:
                return u[:, half * LANES:(half + 1) * LANES]
            return sp_ref[:, POOL_BUF - j, half * LANES:(half + 1) * LANES]
        halves.append(_pool_delta(win, win(0), pos, half))
    d = jnp.concatenate(halves, axis=1)
    ya_o[...] = _dot(d, pw_ref[...]) * ps_ref[...]
    np_o[:, 0:POOL_BUF - 1, :] = sp_ref[:, 1:POOL_BUF, :]
    np_o[:, POOL_BUF - 1, :] = u

    glu = glu_ref[...]
    acc = cb_ref[...] + cw_ref[CONF_K - 1:CONF_K, :] * glu
    for j in range(CONF_K - 1):
        acc = acc + cw_ref[j:j + 1, :] * sc_ref[:, j, :]
    yc_o[...] = _conf_post(acc, avg_ref[...], lng_ref[...], lnb_ref[...], cpw_ref[...])
    nc_o[:, 0:CONF_K - 2, :] = sc_ref[:, 1:CONF_K - 1, :]
    nc_o[:, CONF_K - 2, :] = glu


def _poolconf_step(u, sp_all, glu, sc_all, prev, layer, pw_bd, pscale, cw, cb, lng, lnb, cpw, avg, bt):
    n = u.shape[0]
    row = lambda w: pl.BlockSpec((bt, w), lambda i: (i, 0))
    const3 = lambda a, b_: pl.BlockSpec((None, a, b_), lambda i: (layer, 0, 0))
    state = lambda a: pl.BlockSpec((None, bt) + a.shape[2:], lambda i: (layer, i, 0, 0))
    carry_specs, aliases = _carried(prev, 12, 2)
    return pl.pallas_call(
        functools.partial(_poolconf_step_kernel, pos0=PAST_LEN),
        grid=(n // bt,),
        in_specs=[row(POOL_WIDTH), state(sp_all), row(CONF_WIDTH), state(sc_all),
                  const3(POOL_WIDTH, POOL_WIDTH), const3(1, POOL_WIDTH),
                  const3(CONF_K, CONF_WIDTH), const3(1, CONF_WIDTH),
                  const3(1, CONF_WIDTH), const3(1, CONF_WIDTH),
                  const3(CONF_WIDTH, CONF_WIDTH),
                  pl.BlockSpec((CONF_WIDTH, CONF_WIDTH), lambda i: (0, 0))] + carry_specs,
        out_specs=[row(POOL_WIDTH), row(CONF_WIDTH), state(sp_all), state(sc_all)],
        out_shape=[jax.ShapeDtypeStruct((n, POOL_WIDTH), F32), jax.ShapeDtypeStruct((n, CONF_WIDTH), F32),
                   jax.ShapeDtypeStruct(sp_all.shape, F32), jax.ShapeDtypeStruct(sc_all.shape, F32)],
        input_output_aliases=aliases,
        compiler_params=_cparams(("parallel",)),
        name="pool_conf_step",
    )(u, sp_all, glu, sc_all, pw_bd, pscale, cw, cb, lng, lnb, cpw, avg, *prev)


def _l2norm(x):
    return x * lax.rsqrt(jnp.sum(x * x, axis=-1, keepdims=True) + EPS)


def _gated_norm(o, ng, z):
    return o * lax.rsqrt(jnp.mean(o * o, axis=-1, keepdims=True) + EPS) * ng * _silu(z)


def _tri_masks(c):
    ri = lax.broadcasted_iota(jnp.int32, (c, c), 0)
    ci = lax.broadcasted_iota(jnp.int32, (c, c), 1)
    masks = [ri // SOLVE_BASE == ci // SOLVE_BASE]
    size = SOLVE_BASE
    while size < c:
        masks.append((ri // (2 * size) == ci // (2 * size)) & (ri // size != ci // size))
        size *= 2
    return masks


def _tri_solve(bs, rhss, c, masks):
    ns = [jnp.where(masks[0], b, 0.0) for b in bs]
    ps = [_dot(n, n) for n in ns]
    power = 2
    while power < SOLVE_BASE:
        if 2 * power < SOLVE_BASE:
            sts = [_dot(jnp.concatenate([n, p], axis=0), p) for n, p in zip(ns, ps)]
            ns = [n + p + st[:c] for n, p, st in zip(ns, ps, sts)]
            ps = [st[c:] for st in sts]
        else:
            ns = [n + p + _dot(n, p) for n, p in zip(ns, ps)]
        power *= 2
    for m in masks[1:]:
        ls = [jnp.where(m, b, 0.0) for b in bs]
        tls = [l + _dot(n, l) for n, l in zip(ns, ls)]
        ns = [n + tl + _dot(tl, n) for n, tl in zip(ns, tls)]
    return [rhs + _dot(n, rhs) for n, rhs in zip(ns, rhss)]


def _gdn_kernel(qkv_ref, z_ref, bg_ref, cw_ref, ng_ref, ltri_ref, selb_ref, selg_ref, y_o, s_o,
                eq, eqs, s_scr, *, nb, tt, chunk):
    t = pl.program_id(1)
    width = 3 * DN_WIDTH
    dk = DN_HEAD_DIM

    @pl.when(t == 0)
    def _():
        eq[:, 0:QKV_HIST, :] = jnp.zeros((nb, QKV_HIST, width), F32)
        s_scr[...] = jnp.zeros(s_scr.shape, F32)

    @pl.when(t > 0)
    def _():
        eq[:, 0:QKV_HIST, :] = eq[:, tt:tt + QKV_HIST, :]

    eq[:, QKV_HIST:, :] = qkv_ref[...]

    for r in range(1, DN_CONV):
        eqs[:, r - 1] = eq[:, QKV_HIST - r:QKV_HIST - r + tt, :]

    def conv_cols(bi, lo):
        acc = cw_ref[DN_CONV - 1:DN_CONV, lo:lo + LANES] * eq[bi, QKV_HIST:QKV_HIST + tt, lo:lo + LANES]
        for r in range(1, DN_CONV):
            acc = acc + cw_ref[DN_CONV - 1 - r:DN_CONV - r, lo:lo + LANES] * eqs[bi, r - 1, :, lo:lo + LANES]
        return _silu(acc)

    def split3(x):
        hi = x.astype(BF16)
        r = x - hi.astype(F32)
        mid = r.astype(BF16)
        return hi, mid, (r - mid.astype(F32)).astype(BF16)

    def dot3(a, pieces, left):
        return sum(jnp.dot(a, p, preferred_element_type=F32) if left else
                   jnp.dot(p, a, preferred_element_type=F32) for p in pieces)

    def l2n(x):
        return x * lax.rsqrt(_lane_sum(x * x) + EPS)

    rc = lax.broadcasted_iota(jnp.int32, (chunk, chunk), 0)
    cc = lax.broadcasted_iota(jnp.int32, (chunk, chunk), 1)
    incl = rc >= cc
    strict = rc > cc
    masks = _tri_masks(chunk)
    chunks = range(tt // chunk)
    rows = [slice(c * chunk, (c + 1) * chunk) for c in chunks]

    seqs = [(bi, h) for bi in range(nb) for h in range(DN_HEADS)]
    q, k, v, beta, gcol, grow, eg = [], [], [], [], [], [], []
    for bi in range(nb):
        bg3 = split3(bg_ref[bi])
        gc = dot3(ltri_ref[...], bg3, True)
        gt = gc.T
        beta_d = dot3(selb_ref[...], bg3, False)
        g_d = dot3(selg_ref[...], split3(gc), False)
        eg_d = jnp.exp(g_d)
        for h in range(DN_HEADS):
            cols = slice(h * dk, (h + 1) * dk)
            q.append(l2n(conv_cols(bi, h * dk)) * (dk ** -0.5))
            k.append(l2n(conv_cols(bi, DN_WIDTH + h * dk)))
            v.append(conv_cols(bi, 2 * DN_WIDTH + h * dk))
            beta.append(beta_d[:, cols])
            gcol.append(g_d[:, cols])
            eg.append(eg_d[:, cols])
            grow.append(gt[DN_HEADS + h:DN_HEADS + h + 1, :])
    ns = range(len(seqs))
    qg = [q[s] * eg[s] for s in ns]
    rhs_all = [jnp.concatenate([v[s] * beta[s], k[s] * (beta[s] * eg[s])], axis=1) for s in ns]

    pairs = [(s, c) for s in ns for c in chunks]
    kq = [_dot(jnp.concatenate([k[s][rows[c]], q[s][rows[c]]], axis=0), k[s][rows[c]].T)
          for s, c in pairs]
    decay = [jnp.where(incl, jnp.exp(jnp.where(incl, gcol[s][rows[c]] - grow[s][:, rows[c]], 0.0)), 0.0)
             for s, c in pairs]
    bmat = [jnp.where(strict, -(beta[s][rows[c]] * kq[i][:chunk] * decay[i]), 0.0)
            for i, (s, c) in enumerate(pairs)]
    qk = [kq[i][chunk:] * decay[i] for i in range(len(pairs))]
    sol = _tri_solve(bmat, [rhs_all[s][rows[c]] for s, c in pairs], chunk, masks)
    g_last = [gcol[s][(c + 1) * chunk - 1:(c + 1) * chunk] for s, c in pairs]
    k_tail = [k[s][rows[c]] * jnp.exp(g_last[i] - gcol[s][rows[c]]) for i, (s, c) in enumerate(pairs)]

    state = [s_scr[bi, h] for bi, h in seqs]
    o = [[] for _ in ns]
    for c in chunks:
        idx = [pairs.index((s, c)) for s in ns]
        ws = [_dot(jnp.concatenate([sol[idx[s]][:, dk:], qg[s][rows[c]]], axis=0), state[s]) for s in ns]
        u = [sol[idx[s]][:, :dk] - ws[s][:chunk] for s in ns]
        for s in ns:
            o[s].append(ws[s][chunk:] + _dot(qk[idx[s]], u[s]))
        state = [state[s] * jnp.exp(g_last[idx[s]]) + _dot(k_tail[idx[s]].T, u[s]) for s in ns]
    for s, (bi, h) in enumerate(seqs):
        s_scr[bi, h] = state[s]
        os_ = jnp.concatenate(o[s], axis=0)
        rms = lax.rsqrt(_lane_sum(os_ * os_) * (1.0 / dk) + EPS)
        y_o[bi, :, h * dk:(h + 1) * dk] = os_ * rms * ng_ref[...] * _silu(z_ref[bi, :, h * dk:(h + 1) * dk])

    @pl.when(t == pl.num_programs(1) - 1)
    def _():
        s_o[...] = s_scr[...]


def _gdn(qkv, z, bg, layer, batch, seq, cw, ng, consts, nb, tt):
    assert GDN_CHUNK == LANES and DN_HEAD_DIM == LANES
    blk = lambda w: pl.BlockSpec((nb, tt, w), lambda p, t: (p, t, 0))
    const3 = lambda a, b_: pl.BlockSpec((None, a, b_), lambda p, t: (layer, 0, 0))
    const2 = lambda a, b_: pl.BlockSpec((a, b_), lambda p, t: (0, 0))
    state = (DN_HEADS, DN_HEAD_DIM, DN_HEAD_DIM)
    seq3 = lambda a: a.reshape(batch, seq, a.shape[-1])
    y, s_new = pl.pallas_call(
        functools.partial(_gdn_kernel, nb=nb, tt=tt, chunk=GDN_CHUNK),
        grid=(batch // nb, seq // tt),
        in_specs=[blk(3 * DN_WIDTH), blk(DN_WIDTH), blk(LANES),
                  const3(DN_CONV, 3 * DN_WIDTH), const3(1, DN_HEAD_DIM),
                  const2(tt, tt), const2(LANES, DN_WIDTH), const2(LANES, DN_WIDTH)],
        out_specs=[blk(DN_WIDTH),
                   pl.BlockSpec((nb,) + state, lambda p, t: (p, 0, 0, 0))],
        out_shape=[jax.ShapeDtypeStruct((batch, seq, DN_WIDTH), F32),
                   jax.ShapeDtypeStruct((batch,) + state, F32)],
        scratch_shapes=[pltpu.VMEM((nb, QKV_HIST + tt, 3 * DN_WIDTH), F32),
                        pltpu.VMEM((nb, DN_CONV - 1, tt, 3 * DN_WIDTH), F32),
                        pltpu.VMEM((nb,) + state, F32)],
        compiler_params=_cparams(("parallel", "arbitrary")),
        name="gated_delta",
    )(seq3(qkv), seq3(z), seq3(bg), cw, ng, *consts)
    return y.reshape(batch * seq, DN_WIDTH), s_new


def _gdn_step_kernel(qkv_ref, sq_ref, z_ref, bg_ref, s_ref, cw_ref, ng_ref, *rest):
    y_o, nq_o, s_o = rest[-3:]
    width = 3 * DN_WIDTH
    qkv = qkv_ref[...]
    bt = qkv.shape[0]
    acc = cw_ref[DN_CONV - 1:DN_CONV, :] * qkv
    for j in range(DN_CONV - 1):
        acc = acc + cw_ref[j:j + 1, :] * sq_ref[:, j, :]
    act = _silu(acc)
    nq_o[:, 0:DN_CONV - 2, :] = sq_ref[:, 1:DN_CONV - 1, :]
    nq_o[:, DN_CONV - 2, :] = qkv
    bg = bg_ref[...]
    for h in range(DN_HEADS):
        cols = slice(h * DN_HEAD_DIM, (h + 1) * DN_HEAD_DIM)
        q = _l2norm(act[:, cols]) * (DN_HEAD_DIM ** -0.5)
        k = _l2norm(act[:, DN_WIDTH + h * DN_HEAD_DIM:DN_WIDTH + (h + 1) * DN_HEAD_DIM])
        v = act[:, 2 * DN_WIDTH + h * DN_HEAD_DIM:2 * DN_WIDTH + (h + 1) * DN_HEAD_DIM]
        beta = bg[:, h:h + 1]
        eg = jnp.exp(bg[:, DN_HEADS + h:DN_HEADS + h + 1])
        qk = jnp.sum(q * k, axis=-1, keepdims=True)
        outs = []
        for b in range(bt):
            s = s_ref[b, h]
            k_col = jnp.broadcast_to(k[b:b + 1], (DN_HEAD_DIM, DN_HEAD_DIM)).T
            q_col = jnp.broadcast_to(q[b:b + 1], (DN_HEAD_DIM, DN_HEAD_DIM)).T
            ks = jnp.sum(k_col * s, axis=0, keepdims=True)
            qs = jnp.sum(q_col * s, axis=0, keepdims=True)
            e = eg[b:b + 1]
            u = beta[b:b + 1] * (v[b:b + 1] - e * ks)
            outs.append(e * qs + qk[b:b + 1] * u)
            s_o[b, h] = e * s + k_col * u
        o = jnp.concatenate(outs, axis=0)
        y_o[:, cols] = _gated_norm(o, ng_ref[...], z_ref[:, cols])


def _carried(prev, first_input, first_output):
    specs = [pl.BlockSpec(memory_space=pl.ANY) for _ in prev]
    return specs, {first_input + j: first_output + j for j in range(len(prev))}


def _gdn_step(qkv, sq_all, z, bg, s_all, prev, layer, cw, ng, bt):
    n = qkv.shape[0]
    row = lambda w: pl.BlockSpec((bt, w), lambda i: (i, 0))
    const3 = lambda a, b_: pl.BlockSpec((None, a, b_), lambda i: (layer, 0, 0))
    sblk = pl.BlockSpec((None, bt, DN_HEADS, DN_HEAD_DIM, DN_HEAD_DIM), lambda i: (layer, i, 0, 0, 0))
    qblk = pl.BlockSpec((None, bt) + sq_all.shape[2:], lambda i: (layer, i, 0, 0))
    carry_specs, aliases = _carried(prev, 7, 1)
    return pl.pallas_call(
        _gdn_step_kernel,
        grid=(n // bt,),
        in_specs=[row(3 * DN_WIDTH), qblk, row(DN_WIDTH), row(LANES), sblk,
                  const3(DN_CONV, 3 * DN_WIDTH), const3(1, DN_HEAD_DIM)] + carry_specs,
        out_specs=[row(DN_WIDTH), qblk, sblk],
        out_shape=[jax.ShapeDtypeStruct(z.shape, F32), jax.ShapeDtypeStruct(sq_all.shape, F32),
                   jax.ShapeDtypeStruct(s_all.shape, F32)],
        input_output_aliases=aliases,
        compiler_params=_cparams(("parallel",)),
        name="gated_delta_step",
    )(qkv, sq_all, z, bg, s_all, cw, ng, *prev)


def _mix_ffn_kernel(x_ref, ya_ref, yb_ref, yc_ref, gate1_ref, shift_ref, scale_ref, gate_ref, g_ref,
                    wo_ref, w1_ref, w2_ref, gf_ref, o_ref, *, final, hidden_chunk):
    a_end, b_end = POOL_WIDTH, POOL_WIDTH + DN_WIDTH
    mix = (_dot(ya_ref[...], wo_ref[0:a_end, :]) + _dot(yb_ref[...], wo_ref[a_end:b_end, :])
           + _dot(yc_ref[...], wo_ref[b_end:, :]))
    x = x_ref[...] + gate1_ref[...] * mix
    h = _mod_rmsnorm(x, g_ref[...], scale_ref[...], shift_ref[...]).astype(BF16)
    acc = jnp.zeros(x.shape, F32)
    for c in range(0, D_FF, hidden_chunk):
        a = jnp.maximum(jnp.dot(h, w1_ref[:, c:c + hidden_chunk], preferred_element_type=F32), 0.0)
        acc = acc + jnp.dot((a * a).astype(BF16), w2_ref[c:c + hidden_chunk, :],
                            preferred_element_type=F32)
    y = x + gate_ref[...] * acc
    if final:
        y = y * lax.rsqrt(jnp.mean(y * y, axis=-1, keepdims=True) + EPS) * gf_ref[...]
    o_ref[...] = y


def _mix_ffn(x, ya, yb, yc, mod, mode, layer, rows_per_batch, g2, w_out, w1, w2, g_final, final, tm):
    n = x.shape[0]
    row = lambda w: pl.BlockSpec((tm, w), lambda i: (i, 0))
    resident = lambda a, b: pl.BlockSpec((None, a, b), lambda i: (layer, 0, 0),
                                         pipeline_mode=pl.Buffered(1))
    return pl.pallas_call(
        functools.partial(_mix_ffn_kernel, final=final, hidden_chunk=1024),
        grid=(n // tm,),
        in_specs=[row(D_MODEL), row(POOL_WIDTH), row(DN_WIDTH), row(CONF_WIDTH),
                  _mod_spec(mode, 2, tm, rows_per_batch),
                  _mod_spec(mode, 3, tm, rows_per_batch),
                  _mod_spec(mode, 4, tm, rows_per_batch),
                  _mod_spec(mode, 5, tm, rows_per_batch),
                  pl.BlockSpec((None, 1, D_MODEL), lambda i: (layer, 0, 0)),
                  resident(D_MODEL, D_MODEL), resident(D_MODEL, D_FF), resident(D_FF, D_MODEL),
                  pl.BlockSpec((1, D_MODEL), lambda i: (0, 0))],
        out_specs=row(D_MODEL),
        out_shape=jax.ShapeDtypeStruct(x.shape, F32),
        compiler_params=_cparams(("parallel",)),
        name="mix_ffn",
    )(x, ya, yb, yc, mod, mod, mod, mod, g2, w_out, w1, w2, g_final)


def kernel(x_prompt, x_sample, state_pool, state_qkv_conv, state_delta, state_conv, c_prompt, c_sample,
           w_ada, b_ada, g_norm1, g_norm2, w_in, pool_w, pool_scale, qkv_conv_w, a_log, dt_bias,
           dn_norm_g, conf_dw_w, conf_dw_b, conf_ln_g, conf_ln_b, conf_pw_w, w_out, w_ff1, w_ff2, g_final):
    depth = w_in.shape[0]
    bp, seq, d = x_prompt.shape
    bs = x_sample.shape[0]
    assert d == D_MODEL and x_sample.shape[1] == 1
    tm_p, nb_gdn, tt_gdn, bt_mix, bt_gdn = 512, GDN_SEQS, GDN_TILE, min(32, bs), min(16, bs)
    assert seq % tm_p == 0 and seq % tt_gdn == 0 and tt_gdn % GDN_CHUNK == 0 and bp % nb_gdn == 0
    assert bs % bt_mix == 0 and bs % bt_gdn == 0 and bs % 8 == 0

    w_in_r = w_in.astype(BF16)
    groups = pool_w.shape[1]
    pw_bd = jnp.einsum("lgcd,gh->lgchd", pool_w, jnp.eye(groups, dtype=pool_w.dtype))
    pw_bd = pw_bd.reshape(depth, POOL_WIDTH, POOL_WIDTH).astype(BF16)
    head_of = jnp.arange(CONF_WIDTH) // (CONF_WIDTH // CONF_HEADS)
    avg = jnp.where(head_of[:, None] == head_of[None, :], CONF_HEADS / CONF_WIDTH, 0.0).astype(BF16)
    tpos = jnp.arange(GDN_TILE)
    ltri = ((tpos[:, None] >= tpos[None, :])
            & (tpos[:, None] // GDN_CHUNK == tpos[None, :] // GDN_CHUNK)).astype(BF16)
    lane_id = jnp.arange(LANES)[:, None]
    head_id = jnp.arange(DN_WIDTH)[None, :] // DN_HEAD_DIM
    gdn_consts = (ltri, (lane_id == head_id).astype(BF16), (lane_id == DN_HEADS + head_id).astype(BF16))
    lane_pad = lambda a: jnp.pad(a, ((0, 0), (DN_HEADS, LANES - 2 * DN_HEADS)))[:, None, :]
    alog_row, dtb_row = lane_pad(a_log), lane_pad(dt_bias)
    r3 = lambda a: a[:, None, :]
    g1, g2, pscale, ng = r3(g_norm1), r3(g_norm2), r3(pool_scale), r3(dn_norm_g)
    cb, lng, lnb = r3(conf_dw_b), r3(conf_ln_g), r3(conf_ln_b)
    cpw = conf_pw_w.astype(BF16)
    w_out_b, w1_b, w2_b = w_out.astype(BF16), w_ff1.astype(BF16), w_ff2.astype(BF16)
    gf = g_final[None, :]

    mods = _ada(jnp.concatenate([c_prompt, c_sample], axis=0), w_ada, b_ada)
    mods_p = mods[:, :bp].reshape(depth, bp * 6, 1, d)
    mods_s = mods[:, bp:]

    xp = x_prompt.reshape(bp * seq, d)
    xs = x_sample.reshape(bs, d)
    pc_weights = (pw_bd, pscale, conf_dw_w, cb, lng, lnb, cpw, avg)
    outs = {0: [], 2: [], 4: [], 6: []}
    sp_all, sc_all, sq_all = state_pool, state_conv, state_qkv_conv
    pc_carry = (jnp.zeros_like(sp_all), jnp.zeros_like(sc_all))
    gdn_carry = (jnp.zeros_like(sq_all), jnp.zeros_like(state_delta))
    for l in range(depth):
        last = l == depth - 1
        mp = mods_p[l]
        u, qkv, z, glu, bg, ya, yc = _inproj(xp, mp, "batch", l, seq, g1, w_in_r, alog_row, dtb_row, tm_p,
                                             poolconf_weights=pc_weights)
        yb, s_new = _gdn(qkv, z, bg, l, bp, seq, qkv_conv_w, ng, gdn_consts, nb_gdn, tt_gdn)
        xp = _mix_ffn(xp, ya, yb, yc, mp, "batch", l, seq, g2, w_out_b, w1_b, w2_b, gf, last, tm_p)
        outs[0].append(u.reshape(bp, seq, -1)[:, seq - POOL_BUF:])
        outs[2].append(qkv.reshape(bp, seq, -1)[:, seq - (DN_CONV - 1):])
        outs[4].append(s_new)
        outs[6].append(glu.reshape(bp, seq, -1)[:, seq - (CONF_K - 1):])
        ms = mods_s[l]
        u, qkv, z, glu, bg = _inproj(xs, ms, "row", l, 1, g1, w_in_r, alog_row, dtb_row, bs)
        ya, yc, *pc_carry = _poolconf_step(u, sp_all, glu, sc_all, pc_carry, l, *pc_weights, bt_mix)
        yb, *gdn_carry = _gdn_step(qkv, sq_all, z, bg, state_delta, gdn_carry, l, qkv_conv_w, ng, bt_gdn)
        xs = _mix_ffn(xs, ya, yb, yc, ms, "row", l, 1, g2, w_out_b, w1_b, w2_b, gf, last, bs)
    n_pool, n_conf = pc_carry
    n_qkv, n_delta = gdn_carry
    return (xp.reshape(bp, seq, d), xs.reshape(bs, 1, d),
            jnp.stack(outs[0]), n_pool, jnp.stack(outs[2]), n_qkv,
            jnp.stack(outs[4]), n_delta, jnp.stack(outs[6]), n_conf)
```
